```python
import jax, jax.numpy as jnp
from jax import lax
import numpy as np

D_MODEL = 1024
BATCH = 4
SEQ = 4096
DEPTH = 4

CHUNK = 64
Q_BLOCK = 128
MIX_WIDTH = D_MODEL

RW_HEADS = 8
RW_HEAD_DIM = 64
RW_WIDTH = RW_HEADS * RW_HEAD_DIM
RW_DECAY_LORA = 64
RW_AAA_LORA = 64
RW_GATE_LORA = 128
RW_GN_EPS = 64e-5

MLA_HEADS = 4
MLA_NOPE_DIM = 64
MLA_ROPE_DIM = 32
MLA_V_DIM = 64
MLA_QK_DIM = MLA_NOPE_DIM + MLA_ROPE_DIM
MLA_Q_RANK = 256
MLA_KV_RANK = 128
MLA_WIDTH = MLA_HEADS * MLA_V_DIM
ROPE_THETA = 10000.0

SB_HEADS = 4
SB_HEAD_DIM = 64
SB_WIDTH = SB_HEADS * SB_HEAD_DIM

FFN_HIDDEN = -(-8 * D_MODEL // (3 * 256)) * 256

NORM_EPS = 1e-6
NEG_INF = -1e30

RW_SPLITS = (RW_WIDTH, RW_WIDTH, RW_WIDTH, RW_DECAY_LORA, RW_AAA_LORA, RW_GATE_LORA)
RW_COLS = sum(RW_SPLITS)
REST_SPLITS = (MLA_Q_RANK, MLA_KV_RANK, MLA_ROPE_DIM, SB_WIDTH, SB_WIDTH, SB_WIDTH)
IN_COLS = RW_COLS + sum(REST_SPLITS)

kernel_name = "hybrid_rwkv7_mla_stickbreak_trunk"


def _split_points(sizes):
    return [int(v) for v in np.cumsum(sizes)[:-1]]


def rms_norm(x, g):
    xf = x.astype(jnp.float32)
    y = xf * lax.rsqrt(jnp.mean(xf * xf, axis=-1, keepdims=True) + NORM_EPS)
    return (y * g.astype(jnp.float32)).astype(x.dtype)


def token_shift(p, mu):
    prev = jnp.pad(p, ((0, 0), (1, 0), (0, 0)))[:, :-1]
    return p + mu * (prev - p)


def to_heads(t, n_heads):
    B, T, C = t.shape
    return t.reshape(B, T, n_heads, C // n_heads).transpose(0, 2, 1, 3)


def from_heads(t):
    B, H, T, d = t.shape
    return t.transpose(0, 2, 1, 3).reshape(B, T, H * d)


def rotary_tables(positions):
    inv_freq = ROPE_THETA ** (-jnp.arange(0, MLA_ROPE_DIM, 2, dtype=jnp.float32) / MLA_ROPE_DIM)
    ang = positions.astype(jnp.float32)[..., None] * inv_freq
    return jnp.cos(ang)[:, :, None, :], jnp.sin(ang)[:, :, None, :]


def apply_rotary(x, cos, sin):
    xf = x.astype(jnp.float32)
    x1, x2 = jnp.split(xf, 2, axis=-1)
    return jnp.concatenate([x1 * cos - x2 * sin, x1 * sin + x2 * cos], axis=-1).astype(x.dtype)


def rwkv7_time_mix(r, k, v, wd, ad, gd, w_up, w0, a_up, a0, g_up, k_k, k_a, r_k, ln_g, ln_b):
    B, T, _ = r.shape
    f32 = jnp.float32
    w = -jax.nn.softplus(-(w0 + jnp.tanh(wd) @ w_up)) - 0.5
    decay = jnp.exp(-jnp.exp(w.astype(f32)))
    a = jax.nn.sigmoid(a0 + ad @ a_up)
    g = jax.nn.sigmoid(gd) @ g_up
    heads = lambda t: t.reshape(B, T, RW_HEADS, RW_HEAD_DIM).astype(f32)
    kk = heads(k * k_k)
    kk = kk * lax.rsqrt(jnp.sum(kk * kk, axis=-1, keepdims=True) + 1e-12)
    k = k * (1 + (a - 1) * k_a)
    rh, kh, vh, wh, ah = heads(r), heads(k), heads(v), heads(decay), heads(a)
    a_vec = -kk
    b_vec = kk * ah

    def step(S, inp):
        r_t, w_t, k_t, v_t, a_t, b_t = inp
        sa = jnp.einsum('bhij,bhj->bhi', S, a_t)
        S = S * w_t[:, :, None, :] + sa[..., None] * b_t[:, :, None, :] + v_t[..., None] * k_t[:, :, None, :]
        return S, jnp.einsum('bhij,bhj->bhi', S, r_t)

    seq = tuple(jnp.swapaxes(t, 0, 1) for t in (rh, wh, kh, vh, a_vec, b_vec))
    S0 = jnp.zeros((B, RW_HEADS, RW_HEAD_DIM, RW_HEAD_DIM), f32)
    _, y = lax.scan(step, S0, seq)
    y = jnp.swapaxes(y, 0, 1)
    mu = jnp.mean(y, axis=-1, keepdims=True)
    var = jnp.mean(jnp.square(y - mu), axis=-1, keepdims=True)
    y = ((y - mu) * lax.rsqrt(var + RW_GN_EPS)).reshape(B, T, RW_WIDTH) * ln_g.astype(f32) + ln_b.astype(f32)
    bonus = jnp.sum(rh * kh * r_k.astype(f32), axis=-1, keepdims=True) * vh
    y = y + bonus.reshape(B, T, RW_WIDTH)
    return (y * g.astype(f32)).astype(r.dtype)


def chunk_causal_softmax_attention(q, k, v, scale):
    B, H, T, dk = q.shape
    nb = T // Q_BLOCK
    qb = jnp.moveaxis(q.reshape(B, H, nb, Q_BLOCK, dk), 2, 0)
    k_chunk = jnp.arange(T) // CHUNK

    def one_block(args):
        q_blk, i = args
        q_chunk = (i * Q_BLOCK + jnp.arange(Q_BLOCK)) // CHUNK
        s = jnp.einsum('bhqd,bhkd->bhqk', q_blk, k).astype(jnp.float32) * scale
        s = jnp.where(k_chunk[None, :] <= q_chunk[:, None], s, NEG_INF)
        p = jax.nn.softmax(s, axis=-1).astype(v.dtype)
        return jnp.einsum('bhqk,bhkd->bhqd', p, v)

    out = lax.map(one_block, (qb, jnp.arange(nb)))
    return jnp.moveaxis(out, 0, 2).reshape(B, H, T, v.shape[-1])


def stick_breaking_attention(q, k, v):
    B, H, T, d = q.shape
    nb = T // Q_BLOCK
    qb = jnp.moveaxis(q.reshape(B, H, nb, Q_BLOCK, d), 2, 0)
    k_pos = jnp.arange(T)
    scale = d ** -0.5

    def one_block(args):
        q_blk, i = args
        q_pos = i * Q_BLOCK + jnp.arange(Q_BLOCK)
        strict = k_pos[None, :] < q_pos[:, None]
        z = jnp.einsum('bhqd,bhkd->bhqk', q_blk, k).astype(jnp.float32) * scale
        log_stay = jnp.where(strict, jax.nn.log_sigmoid(-z), 0.0)
        shifted = jnp.pad(log_stay[..., 1:], ((0, 0), (0, 0), (0, 0), (0, 1)))
        log_after = lax.cumsum(shifted, axis=3, reverse=True)
        w = jnp.where(strict, jnp.exp(jax.nn.log_sigmoid(z) + log_after), 0.0)
        return jnp.einsum('bhqk,bhkd->bhqd', w.astype(v.dtype), v)

    out = lax.map(one_block, (qb, jnp.arange(nb)))
    return jnp.moveaxis(out, 0, 2).reshape(B, H, T, d)


def setup_inputs(seed: int = 0) -> dict:
    key = jax.random.key(seed)
    ks = iter(jax.random.split(key, 32))
    f32 = jnp.float32
    L = DEPTH

    def nrm(shape, scale):
        return jax.random.normal(next(ks), shape, f32) * scale

    def gain(shape):
        return 1.0 + nrm(shape, 0.02)

    x = jax.random.normal(next(ks), (BATCH, SEQ, D_MODEL), f32)
    offsets = jax.random.randint(next(ks), (BATCH, 1), 0, 8192, dtype=jnp.int32)
    positions = offsets + jnp.arange(SEQ, dtype=jnp.int32)[None, :]
    return {
        "x": x,
        "positions": positions,
        "attn_norm_g": gain((L, D_MODEL)),
        "w_in": nrm((L, D_MODEL, IN_COLS), D_MODEL ** -0.5),
        "rw_shift_mu": jax.random.uniform(next(ks), (L, RW_COLS), f32),
        "rw_w_up": nrm((L, RW_DECAY_LORA, RW_WIDTH), 0.5 * RW_DECAY_LORA ** -0.5),
        "rw_w0": jax.random.uniform(next(ks), (L, RW_WIDTH), f32, -6.0, 0.0),
        "rw_a_up": nrm((L, RW_AAA_LORA, RW_WIDTH), RW_AAA_LORA ** -0.5),
        "rw_a0": nrm((L, RW_WIDTH), 0.1),
        "rw_g_up": nrm((L, RW_GATE_LORA, RW_WIDTH), RW_GATE_LORA ** -0.5),
        "rw_k_k": 0.85 + nrm((L, RW_WIDTH), 0.02),
        "rw_k_a": gain((L, RW_WIDTH)),
        "rw_r_k": nrm((L, RW_HEADS, RW_HEAD_DIM), 0.1),
        "rw_ln_g": gain((L, RW_WIDTH)),
        "rw_ln_b": nrm((L, RW_WIDTH), 0.02),
        "mla_cq_norm_g": gain((L, MLA_Q_RANK)),
        "mla_ckv_norm_g": gain((L, MLA_KV_RANK)),
        "mla_w_uq": nrm((L, MLA_Q_RANK, MLA_HEADS * MLA_QK_DIM), MLA_Q_RANK ** -0.5),
        "mla_w_ukv": nrm((L, MLA_KV_RANK, MLA_HEADS * (MLA_NOPE_DIM + MLA_V_DIM)), MLA_KV_RANK ** -0.5),
        "mla_q_norm_g": gain((L, MLA_QK_DIM)),
        "mla_k_norm_g": gain((L, MLA_QK_DIM)),
        "w_o": nrm((L, MIX_WIDTH, D_MODEL), MIX_WIDTH ** -0.5),
        "ffn_norm_g": gain((L, D_MODEL)),
        "ffn_w_gate": nrm((L, D_MODEL, FFN_HIDDEN), D_MODEL ** -0.5),
        "ffn_w_up": nrm((L, D_MODEL, FFN_HIDDEN), D_MODEL ** -0.5),
        "ffn_w_down": nrm((L, FFN_HIDDEN, D_MODEL), FFN_HIDDEN ** -0.5),
    }


def reference(x, positions, attn_norm_g, w_in, rw_shift_mu, rw_w_up, rw_w0, rw_a_up, rw_a0, rw_g_up,
              rw_k_k, rw_k_a, rw_r_k, rw_ln_g, rw_ln_b, mla_cq_norm_g, mla_ckv_norm_g, mla_w_uq, mla_w_ukv,
              mla_q_norm_g, mla_k_norm_g, w_o, ffn_norm_g, ffn_w_gate, ffn_w_up, ffn_w_down):
    B, T, _ = x.shape
    cos, sin = rotary_tables(positions)
    rw_pts = _split_points(RW_SPLITS)
    rest_pts = _split_points(REST_SPLITS)
    for l in range(DEPTH):
        h = rms_norm(x, attn_norm_g[l]) @ w_in[l]
        rw_cols = token_shift(h[..., :RW_COLS], rw_shift_mu[l])
        r, k, v, wd, ad, gd = jnp.split(rw_cols, rw_pts, axis=-1)
        c_q, c_kv, k_rope, sb_q, sb_k, sb_v = jnp.split(h[..., RW_COLS:], rest_pts, axis=-1)

        y_rw = rwkv7_time_mix(r, k, v, wd, ad, gd, rw_w_up[l], rw_w0[l], rw_a_up[l], rw_a0[l], rw_g_up[l],
                              rw_k_k[l], rw_k_a[l], rw_r_k[l], rw_ln_g[l], rw_ln_b[l])

        q = (rms_norm(c_q, mla_cq_norm_g[l]) @ mla_w_uq[l]).reshape(B, T, MLA_HEADS, MLA_QK_DIM)
        kv = (rms_norm(c_kv, mla_ckv_norm_g[l]) @ mla_w_ukv[l]).reshape(B, T, MLA_HEADS, MLA_NOPE_DIM + MLA_V_DIM)
        k_nope, v_mla = jnp.split(kv, [MLA_NOPE_DIM], axis=-1)
        k_full = jnp.concatenate(
            [k_nope, jnp.broadcast_to(k_rope[:, :, None, :], (B, T, MLA_HEADS, MLA_ROPE_DIM))], axis=-1)
        q = rms_norm(q, mla_q_norm_g[l])
        k_full = rms_norm(k_full, mla_k_norm_g[l])
        q = jnp.concatenate([q[..., :MLA_NOPE_DIM], apply_rotary(q[..., MLA_NOPE_DIM:], cos, sin)], axis=-1)
        k_full = jnp.concatenate(
            [k_full[..., :MLA_NOPE_DIM], apply_rotary(k_full[..., MLA_NOPE_DIM:], cos, sin)], axis=-1)
        o_mla = chunk_causal_softmax_attention(
            q.transpose(0, 2, 1, 3), k_full.transpose(0, 2, 1, 3), v_mla.transpose(0, 2, 1, 3),
            MLA_QK_DIM ** -0.5)
        y_mla = from_heads(o_mla)

        o_sb = stick_breaking_attention(to_heads(sb_q, SB_HEADS), to_heads(sb_k, SB_HEADS), to_heads(sb_v, SB_HEADS))
        y_sb = from_heads(o_sb)

        mixed = jnp.concatenate([y_rw, y_mla.astype(y_rw.dtype), y_sb.astype(y_rw.dtype)], axis=-1)
        x = x + (mixed @ w_o[l]).astype(x.dtype)

        hn = rms_norm(x, ffn_norm_g[l])
        x = x + ((jax.nn.silu(hn @ ffn_w_gate[l]) * (hn @ ffn_w_up[l])) @ ffn_w_down[l]).astype(x.dtype)
    return x
```

```python
import functools

import jax
import jax.numpy as jnp
import numpy as np
from jax import lax
from jax.experimental import pallas as pl
from jax.experimental.pallas import tpu as pltpu

F32 = jnp.float32
BF16 = jnp.bfloat16

LANES = 128
VMEM_LIMIT_BYTES = 56 * 1024 * 1024

D_MODEL = 1024
CHUNK = 64
CHUNK_SHIFT = 6
RW_HEADS = 8
RW_HEAD_DIM = 64
RW_WIDTH = RW_HEADS * RW_HEAD_DIM
RW_DECAY_LORA = 64
RW_AAA_LORA = 64
RW_GATE_LORA = 128
RW_COLS = 3 * RW_WIDTH + RW_DECAY_LORA + RW_AAA_LORA + RW_GATE_LORA
RW_GN_EPS = 64e-5
MLA_HEADS = 4
MLA_NOPE_DIM = 64
MLA_ROPE_DIM = 32
MLA_V_DIM = 64
MLA_QK_DIM = MLA_NOPE_DIM + MLA_ROPE_DIM
MLA_Q_RANK = 256
MLA_KV_RANK = 128
MLA_WIDTH = MLA_HEADS * MLA_V_DIM
MLA_SLOT = LANES
SB_HEADS = 4
SB_HEAD_DIM = 64
SB_WIDTH = SB_HEADS * SB_HEAD_DIM
FFN_HIDDEN = 2816
ROPE_THETA = 10000.0
NORM_EPS = 1e-6
NEG_INF = -1e30
SB_LOG_FLOOR = -110.0

COL_RW = 0
COL_CQ = RW_COLS
COL_CKV = COL_CQ + MLA_Q_RANK
COL_SBQ = COL_CKV + MLA_KV_RANK
COL_SBK = COL_SBQ + SB_WIDTH
COL_SBV = COL_SBK + SB_WIDTH
COL_ROPE = COL_SBV + SB_WIDTH
IN_COLS_PAD = COL_ROPE + LANES


def _split_bf16(x, parts):
  out = []
  rem = x
  for i in range(parts):
    p = rem.astype(BF16)
    out.append(p)
    if i + 1 < parts:
      rem = rem - p.astype(F32)
  return out


_NN = (((1,), (0,)), ((), ()))
_NT = (((1,), (1,)), ((), ()))
_TN = (((0,), (0,)), ((), ()))


def _dot(a, b, dims=_NN):
  return lax.dot_general(a, b, dims, preferred_element_type=F32)


def _mm(a, b, dims=_NN, pa=2, pb=2):
  as_ = _split_bf16(a, pa) if a.dtype != BF16 else [a]
  bs_ = _split_bf16(b, pb) if b.dtype != BF16 else [b]
  order = max(len(as_), len(bs_))
  acc = None
  for i, ai in enumerate(as_):
    for j, bj in enumerate(bs_):
      if i + j < order:
        t = _dot(ai, bj, dims)
        acc = t if acc is None else acc + t
  return acc


def _sigmoid(x):
  return 1.0 / (1.0 + jnp.exp(-x))


def _softplus(x):
  return jnp.maximum(x, 0.0) + jnp.log1p(jnp.exp(-jnp.abs(x)))


def _iota(shape, dim):
  return lax.broadcasted_iota(jnp.int32, shape, dim)


def _block_ones(n, block):
  shift = block.bit_length() - 1
  r = _iota((n, n), 0) >> shift
  c = _iota((n, n), 1) >> shift
  return jnp.where(r == c, 1.0, 0.0).astype(BF16)


def _seg_sum(x, bd):
  return _mm(x, bd, pa=3, pb=1)


def _rope_kernel(pos_ref, invf_ref, cos_ref, sin_ref):
  ang = pos_ref[0].astype(F32) * invf_ref[...]
  lane = _iota(ang.shape, 1)
  first_half = lane < MLA_NOPE_DIM + MLA_ROPE_DIM // 2
  cos_ref[0] = jnp.cos(ang)
  sin_ref[0] = jnp.where(first_half, -jnp.sin(ang), jnp.sin(ang))


def _rope_tables(positions, tm):
  b, t = positions.shape
  inv_freq = ROPE_THETA ** (-jnp.arange(0, MLA_ROPE_DIM, 2, dtype=F32) / MLA_ROPE_DIM)
  invf = jnp.zeros((1, MLA_SLOT), F32)
  invf = invf.at[0, MLA_NOPE_DIM:MLA_NOPE_DIM + MLA_ROPE_DIM].set(jnp.concatenate([inv_freq, inv_freq]))
  out = jax.ShapeDtypeStruct((b, t, MLA_SLOT), F32)
  return pl.pallas_call(
      _rope_kernel,
      out_shape=(out, out),
      grid=(b, t // tm),
      in_specs=[pl.BlockSpec((1, tm, 1), lambda i, j: (i, j, 0)),
                pl.BlockSpec((1, MLA_SLOT), lambda i, j: (0, 0))],
      out_specs=(pl.BlockSpec((1, tm, MLA_SLOT), lambda i, j: (i, j, 0)),
                 pl.BlockSpec((1, tm, MLA_SLOT), lambda i, j: (i, j, 0))),
      name="rope_tables",
  )(positions.reshape(b, t, 1), invf)


def _rms_scale(x, width):
  ms = jnp.sum(x * x, axis=-1, keepdims=True) * (1.0 / width)
  return lax.rsqrt(ms + NORM_EPS)


def _rotary(x, cos, sin):
  half = MLA_ROPE_DIM // 2
  outs = []
  for h in range(MLA_HEADS):
    xh = x[:, h * MLA_SLOT:(h + 1) * MLA_SLOT]
    lane = _iota(xh.shape, 1)
    up = pltpu.roll(xh, half, 1)
    down = pltpu.roll(xh, MLA_SLOT - half, 1)
    partner = jnp.where(lane >= MLA_NOPE_DIM + half, up, down)
    outs.append(xh * cos + partner * sin)
  return jnp.concatenate(outs, axis=-1)


def _head_rms(x, gain, bd):
  ms = _seg_sum(x * x, bd) * (1.0 / MLA_QK_DIM)
  return x * lax.rsqrt(ms + NORM_EPS) * gain


def _inproj_kernel(x_ref, cos_ref, sin_ref, ng_ref, win_ref, mu_ref, wup_ref, w0_ref, aup_ref, a0_ref,
                   gup_ref, kk_ref, ka_ref, cqg_ref, ckvg_ref, wuq_ref, wuk_ref, wuv_ref, qg_ref, kg_ref,
                   r_out, lw_out, k_out, v_out, kkn_out, b_out, g_out,
                   mq_out, mk_out, mv_out, sq_out, sk_out, sv_out,
                   carry_ref):
  j = pl.program_id(1)
  x = x_ref[0]
  tm = x.shape[0]
  xn = (x * _rms_scale(x, D_MODEL) * ng_ref[...]).astype(BF16)

  h_rw = _dot(xn, win_ref[:, COL_RW:COL_RW + RW_COLS])

  @pl.when(j == 0)
  def _():
    carry_ref[...] = jnp.zeros_like(carry_ref)

  row = _iota(h_rw.shape, 0)
  prev = jnp.where(row == 0, carry_ref[...], pltpu.roll(h_rw, 1, 0))
  carry_ref[...] = h_rw[tm - 1:tm, :]
  sh = h_rw + mu_ref[...] * (prev - h_rw)
  r = sh[:, 0:RW_WIDTH]
  k = sh[:, RW_WIDTH:2 * RW_WIDTH]
  v = sh[:, 2 * RW_WIDTH:3 * RW_WIDTH]
  o = 3 * RW_WIDTH
  wd = sh[:, o:o + RW_DECAY_LORA]
  ad = sh[:, o + RW_DECAY_LORA:o + RW_DECAY_LORA + RW_AAA_LORA]
  gd = sh[:, o + RW_DECAY_LORA + RW_AAA_LORA:RW_COLS]

  w_raw = -_softplus(-(w0_ref[...] + _mm(jnp.tanh(wd), wup_ref[...]))) - 0.5
  lw_out[0] = -jnp.exp(w_raw)
  a = _sigmoid(a0_ref[...] + _mm(ad, aup_ref[...]))
  g_out[0] = _mm(_sigmoid(gd), gup_ref[...])
  kk = k * kk_ref[...]
  bd64 = _block_ones(RW_WIDTH, RW_HEAD_DIM)
  kkn = kk * lax.rsqrt(_seg_sum(kk * kk, bd64) + 1e-12)
  r_out[0] = r
  k_out[0] = k * (1.0 + (a - 1.0) * ka_ref[...])
  v_out[0] = v
  kkn_out[0] = kkn
  b_out[0] = kkn * a

  c_q = _dot(xn, win_ref[:, COL_CQ:COL_CQ + MLA_Q_RANK])
  c_kv = _dot(xn, win_ref[:, COL_CKV:COL_CKV + MLA_KV_RANK])
  k_rope = _dot(xn, win_ref[:, COL_ROPE:COL_ROPE + LANES])
  cqn = (c_q * _rms_scale(c_q, MLA_Q_RANK) * cqg_ref[...]).astype(BF16)
  ckvn = (c_kv * _rms_scale(c_kv, MLA_KV_RANK) * ckvg_ref[...]).astype(BF16)
  bd128 = _block_ones(MLA_HEADS * MLA_SLOT, MLA_SLOT)
  cos = cos_ref[0]
  sin = sin_ref[0]
  q = _head_rms(_dot(cqn, wuq_ref[...]), qg_ref[...], bd128)
  kf = _dot(ckvn, wuk_ref[...]) + jnp.concatenate([k_rope] * MLA_HEADS, axis=-1)
  kf = _head_rms(kf, kg_ref[...], bd128)
  mq_out[0] = (_rotary(q, cos, sin) * (MLA_QK_DIM ** -0.5)).astype(BF16)
  mk_out[0] = _rotary(kf, cos, sin).astype(BF16)
  mv_out[0] = _dot(ckvn, wuv_ref[...]).astype(BF16)

  sq_out[0] = (_dot(xn, win_ref[:, COL_SBQ:COL_SBQ + SB_WIDTH]) * (SB_HEAD_DIM ** -0.5)).astype(BF16)
  sk_out[0] = _dot(xn, win_ref[:, COL_SBK:COL_SBK + SB_WIDTH]).astype(BF16)
  sv_out[0] = _dot(xn, win_ref[:, COL_SBV:COL_SBV + SB_WIDTH]).astype(BF16)


def _inproj(x, cos, sin, lw, tm):
  b, t, _ = x.shape
  row = lambda i, j: (i, j, 0)
  const = lambda i, j: (0, 0)
  full = lambda a: pl.BlockSpec(a.shape, const)
  weights = [lw[n] for n in ("ng", "win", "mu", "wup", "w0", "aup", "a0", "gup", "kk", "ka",
                             "cqg", "ckvg", "wuq", "wuk", "wuv", "qg", "kg")]
  rw = jax.ShapeDtypeStruct((b, t, RW_WIDTH), F32)
  att4 = jax.ShapeDtypeStruct((b, t, MLA_HEADS * MLA_SLOT), BF16)
  att2 = jax.ShapeDtypeStruct((b, t, SB_WIDTH), BF16)
  out_shape = (rw,) * 7 + (att4, att4, att2, att2, att2, att2)
  out_specs = tuple(pl.BlockSpec((1, tm, s.shape[-1]), row) for s in out_shape)
  return pl.pallas_call(
      _inproj_kernel,
      out_shape=out_shape,
      grid=(b, t // tm),
      in_specs=[pl.BlockSpec((1, tm, D_MODEL), row),
                pl.BlockSpec((1, tm, MLA_SLOT), row),
                pl.BlockSpec((1, tm, MLA_SLOT), row)] + [full(w) for w in weights],
      out_specs=out_specs,
      scratch_shapes=[pltpu.VMEM((1, RW_COLS), F32)],
      compiler_params=pltpu.CompilerParams(
          dimension_semantics=("arbitrary", "arbitrary"), vmem_limit_bytes=VMEM_LIMIT_BYTES),
      name="inproj",
  )(x, cos, sin, *weights)


def _unit_lower_inverse(n):
  size = n.shape[0]
  r = _iota((size, size), 0)
  c = _iota((size, size), 1)
  eye = jnp.where(r == c, 1.0, 0.0).astype(F32)
  nd = jnp.where((r >> 3) == (c >> 3), n, 0.0)
  n2 = _mm(nd, nd)
  n4 = _mm(n2, n2)
  p1 = eye + nd + n2 + _mm(nd, n2)
  inv = p1 + _mm(p1, n4)
  shift = 3
  while (1 << shift) < CHUNK:
    off = jnp.where(((r >> (shift + 1)) == (c >> (shift + 1))) & ((r >> shift) != (c >> shift)), n, 0.0)
    inv = inv + _mm(_mm(inv, off), inv)
    shift += 1
  return inv


def _rwkv_pair_chunk(at, rt, bt, kt, v, s, p_end):
  c = CHUNK
  lane = _iota((c, LANES), 1)
  lo = lane < RW_HEAD_DIM
  stack = lambda m: jnp.concatenate([jnp.where(lo, m, 0.0), jnp.where(lo, 0.0, m)], axis=0)
  a_st, r_st, b_st, k_st, v_st = stack(at), stack(rt), stack(bt), stack(kt), stack(v)
  lhs = jnp.concatenate([a_st, r_st], axis=0)
  rhs = jnp.concatenate([b_st, k_st], axis=0)
  sc = _mm(lhs, rhs, _NT)
  rr = _iota(sc.shape, 0)
  cc = _iota(sc.shape, 1)
  keep = ((rr & (c - 1)) + jnp.where(rr >= 2 * c, 1, 0)) > (cc & (c - 1))
  sc = jnp.where(keep, sc, 0.0)
  n_ab = sc[0:2 * c, 0:2 * c]
  a_ak = sc[0:2 * c, 2 * c:4 * c]
  r_bk = sc[2 * c:4 * c, :]
  inv = _unit_lower_inverse(n_ab)
  x = _mm(a_st, s, _NT) + _mm(a_ak, v_st)
  u = _mm(inv, x)
  uv = jnp.concatenate([u, v_st], axis=0)
  y_st = _mm(r_st, s, _NT) + _mm(r_bk, uv)
  y = y_st[0:c] + y_st[c:2 * c]
  s_new = (s + _mm(uv, rhs, _TN)) * p_end
  return y, s_new


def _rwkv_kernel(r_ref, lw_ref, k_ref, v_ref, kkn_ref, b_ref, g_ref, rk_ref, lng_ref, lnb_ref,
                 o_ref, s_ref, y_ref):
  j = pl.program_id(1)
  pairs = RW_WIDTH // LANES

  @pl.when(j == 0)
  def _():
    s_ref[...] = jnp.zeros_like(s_ref)

  tc = r_ref.shape[1]
  rr = _iota((CHUNK, CHUNK), 0)
  cc = _iota((CHUNK, CHUNK), 1)
  tri = jnp.where(rr >= cc, 1.0, 0.0).astype(BF16)
  for ci in range(tc // CHUNK):
    rows = slice(ci * CHUNK, (ci + 1) * CHUNK)
    lw = lw_ref[0, rows, :]
    lp = _mm(tri, lw, pa=1, pb=3)
    e_in = jnp.exp(lp)
    e_ex = jnp.exp(lp - lw)
    e_neg = jnp.exp(-lp)
    at = -kkn_ref[0, rows, :] * e_ex
    rt = r_ref[0, rows, :] * e_in
    bt = b_ref[0, rows, :] * e_neg
    kt = k_ref[0, rows, :] * e_neg
    v = v_ref[0, rows, :]
    p_end = e_in[CHUNK - 1:CHUNK, :]
    for p in range(pairs):
      ln = slice(p * LANES, (p + 1) * LANES)
      y, s_new = _rwkv_pair_chunk(at[:, ln], rt[:, ln], bt[:, ln], kt[:, ln], v[:, ln], s_ref[p], p_end[:, ln])
      s_ref[p] = s_new
      y_ref[rows, ln] = y

  y = y_ref[...]
  bd = _block_ones(RW_WIDTH, RW_HEAD_DIM)
  inv_d = 1.0 / RW_HEAD_DIM
  mean = _seg_sum(y, bd) * inv_d
  d = y - mean
  var = _seg_sum(d * d, bd) * inv_d
  yn = d * lax.rsqrt(var + RW_GN_EPS) * lng_ref[...] + lnb_ref[...]
  bonus = _seg_sum(r_ref[0] * k_ref[0] * rk_ref[...], bd) * v_ref[0]
  o_ref[0] = ((yn + bonus) * g_ref[0]).astype(BF16)


def _rwkv(r, lw, k, v, kkn, bvec, g, rk, lng, lnb, tc):
  b, t, _ = r.shape
  row = lambda i, j: (i, j, 0)
  const = lambda i, j: (0, 0)
  seq = pl.BlockSpec((1, tc, RW_WIDTH), row)
  vec = pl.BlockSpec((1, RW_WIDTH), const)
  return pl.pallas_call(
      _rwkv_kernel,
      out_shape=jax.ShapeDtypeStruct((b, t, RW_WIDTH), BF16),
      grid=(b, t // tc),
      in_specs=[seq] * 7 + [vec] * 3,
      out_specs=seq,
      scratch_shapes=[pltpu.VMEM((RW_WIDTH // LANES, LANES, LANES), F32),
                      pltpu.VMEM((tc, RW_WIDTH), F32)],
      compiler_params=pltpu.CompilerParams(
          dimension_semantics=("arbitrary", "arbitrary"), vmem_limit_bytes=VMEM_LIMIT_BYTES),
      name="rwkv",
  )(r, lw, k, v, kkn, bvec, g, rk, lng, lnb)


def _mla_kernel(q_ref, k_ref, v_ref, o_ref, *, tq, tk):
  i = pl.program_id(2)
  lane = _iota((tq, LANES), 1)
  lo = lane < MLA_V_DIM
  outs = []
  for h in range(2):
    q = q_ref[0, :, h * MLA_SLOT:(h + 1) * MLA_SLOT]

    def block(jb, carry, masked, q=q, h=h):
      m, l, acc = carry
      ks = pl.multiple_of(jb * tk, tk)
      kb = k_ref[0, pl.ds(ks, tk), h * MLA_SLOT:(h + 1) * MLA_SLOT]
      vb = v_ref[0, pl.ds(ks, tk), :]
      s = _dot(q, kb, _NT)
      if masked:
        qc = (i * tq + _iota((tq, tk), 0)) >> CHUNK_SHIFT
        kc = (jb * tk + _iota((tq, tk), 1)) >> CHUNK_SHIFT
        s = jnp.where(kc <= qc, s, NEG_INF)
      m_new = jnp.maximum(m, jnp.max(s, axis=-1, keepdims=True))
      alpha = jnp.exp(m - m_new)
      p = jnp.exp(s - m_new)
      l = alpha * l + jnp.sum(p, axis=-1, keepdims=True)
      acc = alpha * acc + _dot(p.astype(BF16), vb)
      return m_new, l, acc

    init = (jnp.full((tq, 1), NEG_INF, F32), jnp.zeros((tq, 1), F32), jnp.zeros((tq, LANES), F32))
    nfull = i * (tq // tk)
    carry = lax.fori_loop(0, nfull, functools.partial(block, masked=False), init)
    for d in range(tq // tk):
      carry = block(nfull + d, carry, True)
    m, l, acc = carry
    outs.append(acc / l)
  o_ref[0] = jnp.where(lo, outs[0], outs[1]).astype(BF16)


def _mla(q, k, v, tq, tk):
  b, t, _ = q.shape
  pairs = MLA_HEADS // 2
  return pl.pallas_call(
      functools.partial(_mla_kernel, tq=tq, tk=tk),
      out_shape=jax.ShapeDtypeStruct((b, t, MLA_WIDTH), BF16),
      grid=(b, pairs, t // tq),
      in_specs=[pl.BlockSpec((1, tq, 2 * MLA_SLOT), lambda bi, p, i: (bi, i, p)),
                pl.BlockSpec((1, t, 2 * MLA_SLOT), lambda bi, p, i: (bi, 0, p)),
                pl.BlockSpec((1, t, LANES), lambda bi, p, i: (bi, 0, p))],
      out_specs=pl.BlockSpec((1, tq, LANES), lambda bi, p, i: (bi, i, p)),
      compiler_params=pltpu.CompilerParams(
          dimension_semantics=("arbitrary", "arbitrary", "arbitrary"), vmem_limit_bytes=VMEM_LIMIT_BYTES),
      name="mla_attention",
  )(q, k, v)


def _sb_kernel(q_ref, k_ref, v_ref, o_ref, *, tq, tk):
  i = pl.program_id(2)
  lane = _iota((tq, LANES), 1)
  lo = lane < SB_HEAD_DIM
  qp = q_ref[0]
  zero = jnp.zeros_like(qp)
  qs = (jnp.where(lo, qp, zero), jnp.where(lo, zero, qp))
  rr = _iota((tk, tk + LANES), 0)
  cc = _iota((tk, tk + LANES), 1)
  tri = jnp.where((rr >= cc) | (cc >= tk), 1.0, 0.0).astype(BF16)

  def cond(state):
    jb, run0, run1, _, _ = state
    live = jnp.maximum(jnp.max(run0), jnp.max(run1)) > SB_LOG_FLOOR
    return (jb >= 0) & live

  def body(state):
    jb, run0, run1, acc0, acc1 = state
    ks = pl.multiple_of(jb * tk, tk)
    kb = k_ref[0, pl.ds(ks, tk), :]
    vb = v_ref[0, pl.ds(ks, tk), :]
    qpos = i * tq + _iota((tq, tk), 0)
    kpos = jb * tk + _iota((tq, tk), 1)
    strict = kpos < qpos
    new = []
    for q, run, acc in ((qs[0], run0, acc0), (qs[1], run1, acc1)):
      z = _dot(q, kb, _NT)
      ls = jnp.where(strict, -_softplus(z), 0.0)
      cs = _mm(ls, tri, pa=2, pb=1)
      logw = z + cs[:, 0:tk] + jnp.concatenate([run] * (tk // LANES), axis=-1)
      w = jnp.where(strict, jnp.exp(logw), 0.0)
      new.append((run + cs[:, tk:tk + LANES], acc + _dot(w.astype(BF16), vb)))
    return jb - 1, new[0][0], new[1][0], new[0][1], new[1][1]

  zr = jnp.zeros((tq, LANES), F32)
  start = (i + 1) * (tq // tk) - 1
  _, _, _, acc0, acc1 = lax.while_loop(cond, body, (start, zr, zr, zr, zr))
  o_ref[0] = jnp.where(lo, acc0, acc1).astype(BF16)


def _sb(q, k, v, tq, tk):
  b, t, _ = q.shape
  pairs = SB_HEADS // 2
  return pl.pallas_call(
      functools.partial(_sb_kernel, tq=tq, tk=tk),
      out_shape=jax.ShapeDtypeStruct((b, t, SB_WIDTH), BF16),
      grid=(b, pairs, t // tq),
      in_specs=[pl.BlockSpec((1, tq, LANES), lambda bi, p, i: (bi, i, p)),
                pl.BlockSpec((1, t, LANES), lambda bi, p, i: (bi, 0, p)),
                pl.BlockSpec((1, t, LANES), lambda bi, p, i: (bi, 0, p))],
      out_specs=pl.BlockSpec((1, tq, LANES), lambda bi, p, i: (bi, i, p)),
      compiler_params=pltpu.CompilerParams(
          dimension_semantics=("arbitrary", "arbitrary", "arbitrary"), vmem_limit_bytes=VMEM_LIMIT_BYTES),
      name="sb_attention",
  )(q, k, v)


def _outffn_kernel(x_ref, yrw_ref, ymla_ref, ysb_ref, wo_ref, ng_ref, wg_ref, wu_ref, wd_ref,
                   o_ref, x1_ref, hn_ref, acc_ref):
  j = pl.program_id(1)

  @pl.when(j == 0)
  def _():
    mix = (_dot(yrw_ref[...], wo_ref[0:RW_WIDTH, :])
           + _dot(ymla_ref[...], wo_ref[RW_WIDTH:RW_WIDTH + MLA_WIDTH, :])
           + _dot(ysb_ref[...], wo_ref[RW_WIDTH + MLA_WIDTH:, :]))
    x1 = x_ref[...] + mix
    x1_ref[...] = x1
    hn_ref[...] = (x1 * _rms_scale(x1, D_MODEL) * ng_ref[...]).astype(BF16)
    acc_ref[...] = jnp.zeros_like(acc_ref)

  hn = hn_ref[...]
  gate = _dot(hn, wg_ref[...])
  up = _dot(hn, wu_ref[...])
  act = (gate * _sigmoid(gate) * up).astype(BF16)
  acc_ref[...] += _dot(act, wd_ref[...])

  @pl.when(j == pl.num_programs(1) - 1)
  def _():
    o_ref[...] = x1_ref[...] + acc_ref[...]


def _outffn(x, yrw, ymla, ysb, lw, tm, th):
  n = x.shape[0]
  row = lambda i, j: (i, 0)
  const = lambda i, j: (0, 0)
  return pl.pallas_call(
      _outffn_kernel,
      out_shape=jax.ShapeDtypeStruct((n, D_MODEL), F32),
      grid=(n // tm, FFN_HIDDEN // th),
      in_specs=[pl.BlockSpec((tm, D_MODEL), row),
                pl.BlockSpec((tm, RW_WIDTH), row),
                pl.BlockSpec((tm, MLA_WIDTH), row),
                pl.BlockSpec((tm, SB_WIDTH), row),
                pl.BlockSpec((D_MODEL, D_MODEL), const),
                pl.BlockSpec((1, D_MODEL), const),
                pl.BlockSpec((D_MODEL, th), lambda i, j: (0, j)),
                pl.BlockSpec((D_MODEL, th), lambda i, j: (0, j)),
                pl.BlockSpec((th, D_MODEL), lambda i, j: (j, 0))],
      out_specs=pl.BlockSpec((tm, D_MODEL), row),
      scratch_shapes=[pltpu.VMEM((tm, D_MODEL), F32),
                      pltpu.VMEM((tm, D_MODEL), BF16),
                      pltpu.VMEM((tm, D_MODEL), F32)],
      compiler_params=pltpu.CompilerParams(
          dimension_semantics=("arbitrary", "arbitrary"), vmem_limit_bytes=VMEM_LIMIT_BYTES),
      name="outproj_ffn",
  )(x, yrw, ymla, ysb, lw["wo"], lw["fg"], lw["wg"], lw["wu"], lw["wd"])


def _pad_heads(w, heads, dim, slot):
  lead = w.shape[:-1]
  w = w.reshape(lead + (heads, dim))
  w = jnp.pad(w, [(0, 0)] * len(lead) + [(0, 0), (0, slot - dim)])
  return w.reshape(lead + (heads * slot,))


def _layer_weights(l, attn_norm_g, w_in, rw_shift_mu, rw_w_up, rw_w0, rw_a_up, rw_a0, rw_g_up, rw_k_k,
                   rw_k_a, rw_r_k, rw_ln_g, rw_ln_b, mla_cq_norm_g, mla_ckv_norm_g, mla_w_uq, mla_w_ukv,
                   mla_q_norm_g, mla_k_norm_g, w_o, ffn_norm_g, ffn_w_gate, ffn_w_up, ffn_w_down):
  vec = lambda a: a[l].reshape(1, -1)
  win = w_in[l]
  rest = RW_COLS
  c_q = win[:, rest:rest + MLA_Q_RANK]
  c_kv = win[:, rest + MLA_Q_RANK:rest + MLA_Q_RANK + MLA_KV_RANK]
  o = rest + MLA_Q_RANK + MLA_KV_RANK
  k_rope = win[:, o:o + MLA_ROPE_DIM]
  sb = win[:, o + MLA_ROPE_DIM:]
  rope_slot = jnp.pad(k_rope, ((0, 0), (MLA_NOPE_DIM, LANES - MLA_NOPE_DIM - MLA_ROPE_DIM)))
  win_p = jnp.concatenate([win[:, :RW_COLS], c_q, c_kv, sb, rope_slot], axis=1).astype(BF16)
  ukv = mla_w_ukv[l].reshape(MLA_KV_RANK, MLA_HEADS, MLA_NOPE_DIM + MLA_V_DIM)
  wuk = _pad_heads(ukv[:, :, :MLA_NOPE_DIM].reshape(MLA_KV_RANK, -1), MLA_HEADS, MLA_NOPE_DIM, MLA_SLOT)
  wuv = ukv[:, :, MLA_NOPE_DIM:].reshape(MLA_KV_RANK, MLA_HEADS * MLA_V_DIM)
  head_gain = lambda g: jnp.tile(jnp.pad(g[l], (0, MLA_SLOT - MLA_QK_DIM)), MLA_HEADS).reshape(1, -1)
  return dict(
      ng=vec(attn_norm_g), win=win_p, mu=vec(rw_shift_mu), wup=rw_w_up[l], w0=vec(rw_w0),
      aup=rw_a_up[l], a0=vec(rw_a0), gup=rw_g_up[l], kk=vec(rw_k_k), ka=vec(rw_k_a),
      rk=vec(rw_r_k), lng=vec(rw_ln_g), lnb=vec(rw_ln_b),
      cqg=vec(mla_cq_norm_g), ckvg=vec(mla_ckv_norm_g),
      wuq=_pad_heads(mla_w_uq[l], MLA_HEADS, MLA_QK_DIM, MLA_SLOT).astype(BF16),
      wuk=wuk.astype(BF16), wuv=wuv.astype(BF16),
      qg=head_gain(mla_q_norm_g), kg=head_gain(mla_k_norm_g),
      wo=w_o[l].astype(BF16), fg=vec(ffn_norm_g),
      wg=ffn_w_gate[l].astype(BF16), wu=ffn_w_up[l].astype(BF16), wd=ffn_w_down[l].astype(BF16))


def kernel(x, positions, attn_norm_g, w_in, rw_shift_mu, rw_w_up, rw_w0, rw_a_up, rw_a0, rw_g_up, rw_k_k, rw_k_a, rw_r_k, rw_ln_g, rw_ln_b, mla_cq_norm_g, mla_ckv_norm_g, mla_w_uq, mla_w_ukv, mla_q_norm_g, mla_k_norm_g, w_o, ffn_norm_g, ffn_w_gate, ffn_w_up, ffn_w_down):
  b, t, d = x.shape
  depth = w_in.shape[0]
  params = (attn_norm_g, w_in, rw_shift_mu, rw_w_up, rw_w0, rw_a_up, rw_a0, rw_g_up, rw_k_k, rw_k_a, rw_r_k,
            rw_ln_g, rw_ln_b, mla_cq_norm_g, mla_ckv_norm_g, mla_w_uq, mla_w_ukv, mla_q_norm_g, mla_k_norm_g,
            w_o, ffn_norm_g, ffn_w_gate, ffn_w_up, ffn_w_down)
  tm = min(256, t)
  cos, sin = _rope_tables(positions, tm)
  for l in range(depth):
    lw = _layer_weights(l, *params)
    (r, lwd, k, v, kkn, bvec, g, mq, mk, mv, sq, sk, sv) = _inproj(x, cos, sin, lw, tm)
    y_rw = _rwkv(r, lwd, k, v, kkn, bvec, g, lw["rk"], lw["lng"], lw["lnb"], tc=min(256, t))
    y_mla = _mla(mq, mk, mv, tq=min(256, t), tk=min(256, t))
    y_sb = _sb(sq, sk, sv, tq=min(256, t), tk=LANES)
    x = _outffn(x.reshape(b * t, d), y_rw.reshape(b * t, -1), y_mla.reshape(b * t, -1),
                y_sb.reshape(b * t, -1), lw, tm=256, th=FFN_HIDDEN // 2).reshape(b, t, d)
  return x
```

```python
import functools

import jax
import jax.numpy as jnp
import numpy as np
from jax import lax
from jax.experimental import pallas as pl
from jax.experimental.pallas import tpu as pltpu

F32 = jnp.float32
BF16 = jnp.bfloat16

LANES = 128
VMEM_LIMIT_BYTES = 56 * 1024 * 1024

D_MODEL = 1024
CHUNK = 64
CHUNK_SHIFT = 6
RW_HEADS = 8
RW_HEAD_DIM = 64
RW_WIDTH = RW_HEADS * RW_HEAD_DIM
RW_DECAY_LORA = 64
RW_AAA_LORA = 64
RW_GATE_LORA = 128
RW_COLS = 3 * RW_WIDTH + RW_DECAY_LORA + RW_AAA_LORA + RW_GATE_LORA
RW_GN_EPS = 64e-5
MLA_HEADS = 4
MLA_NOPE_DIM = 64
MLA_ROPE_DIM = 32
MLA_V_DIM = 64
MLA_QK_DIM = MLA_NOPE_DIM + MLA_ROPE_DIM
MLA_Q_RANK = 256
MLA_KV_RANK = 128
MLA_WIDTH = MLA_HEADS * MLA_V_DIM
MLA_SLOT = LANES
SB_HEADS = 4
SB_HEAD_DIM = 64
SB_WIDTH = SB_HEADS * SB_HEAD_DIM
FFN_HIDDEN = 2816
ROPE_THETA = 10000.0
NORM_EPS = 1e-6
NEG_INF = -1e30
SB_LOG_FLOOR = -110.0

COL_RW = 0
COL_CQ = RW_COLS
COL_CKV = COL_CQ + MLA_Q_RANK
COL_SBQ = COL_CKV + MLA_KV_RANK
COL_SBK = COL_SBQ + SB_WIDTH
COL_SBV = COL_SBK + SB_WIDTH
COL_ROPE = COL_SBV + SB_WIDTH
IN_COLS_PAD = COL_ROPE + LANES


def _split_bf16(x, parts):
  out = []
  rem = x
  for i in range(parts):
    p = rem.astype(BF16)
    out.append(p)
    if i + 1 < parts:
      rem = rem - p.astype(F32)
  return out


_NN = (((1,), (0,)), ((), ()))
_NT = (((1,), (1,)), ((), ()))
_TN = (((0,), (0,)), ((), ()))


def _dot(a, b, dims=_NN):
  return lax.dot_general(a, b, dims, preferred_element_type=F32)


def _mm(a, b, dims=_NN, pa=2, pb=2):
  as_ = _split_bf16(a, pa) if a.dtype != BF16 else [a]
  bs_ = _split_bf16(b, pb) if b.dtype != BF16 else [b]
  order = max(len(as_), len(bs_))
  acc = None
  for i, ai in enumerate(as_):
    for j, bj in enumerate(bs_):
      if i + j < order:
        t = _dot(ai, bj, dims)
        acc = t if acc is None else acc + t
  return acc


def _sigmoid(x):
  return 1.0 / (1.0 + jnp.exp(-x))


def _softplus(x):
  return jnp.maximum(x, 0.0) + jnp.log(1.0 + jnp.exp(-jnp.abs(x)))


def _iota(shape, dim):
  return lax.broadcasted_iota(jnp.int32, shape, dim)


def _block_ones(n, block):
  shift = block.bit_length() - 1
  r = _iota((n, n), 0) >> shift
  c = _iota((n, n), 1) >> shift
  return jnp.where(r == c, 1.0, 0.0).astype(BF16)


def _seg_sum(x, bd):
  return _mm(x, bd, pa=3, pb=1)


def _rope_kernel(pos_ref, invf_ref, cos_ref, sin_ref):
  ang = pos_ref[0].astype(F32) * invf_ref[...]
  lane = _iota(ang.shape, 1)
  first_half = lane < MLA_NOPE_DIM + MLA_ROPE_DIM // 2
  cos_ref[0] = jnp.cos(ang)
  sin_ref[0] = jnp.where(first_half, -jnp.sin(ang), jnp.sin(ang))


def _rope_tables(positions, tm):
  b, t = positions.shape
  inv_freq = ROPE_THETA ** (-jnp.arange(0, MLA_ROPE_DIM, 2, dtype=F32) / MLA_ROPE_DIM)
  invf = jnp.zeros((1, MLA_SLOT), F32)
  invf = invf.at[0, MLA_NOPE_DIM:MLA_NOPE_DIM + MLA_ROPE_DIM].set(jnp.concatenate([inv_freq, inv_freq]))
  out = jax.ShapeDtypeStruct((b, t, MLA_SLOT), F32)
  return pl.pallas_call(
      _rope_kernel,
      out_shape=(out, out),
      grid=(b, t // tm),
      in_specs=[pl.BlockSpec((1, tm, 1), lambda i, j: (i, j, 0)),
                pl.BlockSpec((1, MLA_SLOT), lambda i, j: (0, 0))],
      out_specs=(pl.BlockSpec((1, tm, MLA_SLOT), lambda i, j: (i, j, 0)),
                 pl.BlockSpec((1, tm, MLA_SLOT), lambda i, j: (i, j, 0))),
      name="rope_tables",
  )(positions.reshape(b, t, 1), invf)


def _rms_scale(x, width):
  ms = jnp.sum(x * x, axis=-1, keepdims=True) * (1.0 / width)
  return lax.rsqrt(ms + NORM_EPS)


def _rotary(x, cos, sin):
  half = MLA_ROPE_DIM // 2
  outs = []
  for h in range(MLA_HEADS):
    xh = x[:, h * MLA_SLOT:(h + 1) * MLA_SLOT]
    lane = _iota(xh.shape, 1)
    up = pltpu.roll(xh, half, 1)
    down = pltpu.roll(xh, MLA_SLOT - half, 1)
    partner = jnp.where(lane >= MLA_NOPE_DIM + half, up, down)
    outs.append(xh * cos + partner * sin)
  return jnp.concatenate(outs, axis=-1)


def _head_rms(x, gain, bd):
  ms = _seg_sum(x * x, bd) * (1.0 / MLA_QK_DIM)
  return x * lax.rsqrt(ms + NORM_EPS) * gain


def _inproj_kernel(x_ref, cos_ref, sin_ref, ng_ref, win_ref, mu_ref, wup_ref, w0_ref, aup_ref, a0_ref,
                   gup_ref, kk_ref, ka_ref, cqg_ref, ckvg_ref, wuq_ref, wuk_ref, wuv_ref, qg_ref, kg_ref,
                   r_out, lw_out, k_out, v_out, kkn_out, b_out, g_out,
                   mq_out, mk_out, mv_out, sq_out, sk_out, sv_out,
                   carry_ref):
  j = pl.program_id(1)
  x = x_ref[0]
  tm = x.shape[0]
  xn = (x * _rms_scale(x, D_MODEL) * ng_ref[...]).astype(BF16)

  h_rw = _dot(xn, win_ref[:, COL_RW:COL_RW + RW_COLS])

  @pl.when(j == 0)
  def _():
    carry_ref[...] = jnp.zeros_like(carry_ref)

  row = _iota(h_rw.shape, 0)
  prev = jnp.where(row == 0, carry_ref[...], pltpu.roll(h_rw, 1, 0))
  carry_ref[...] = h_rw[tm - 1:tm, :]
  sh = h_rw + mu_ref[...] * (prev - h_rw)
  r = sh[:, 0:RW_WIDTH]
  k = sh[:, RW_WIDTH:2 * RW_WIDTH]
  v = sh[:, 2 * RW_WIDTH:3 * RW_WIDTH]
  o = 3 * RW_WIDTH
  wd = sh[:, o:o + RW_DECAY_LORA]
  ad = sh[:, o + RW_DECAY_LORA:o + RW_DECAY_LORA + RW_AAA_LORA]
  gd = sh[:, o + RW_DECAY_LORA + RW_AAA_LORA:RW_COLS]

  w_raw = -_softplus(-(w0_ref[...] + _mm(jnp.tanh(wd), wup_ref[...]))) - 0.5
  lw_out[0] = -jnp.exp(w_raw)
  a = _sigmoid(a0_ref[...] + _mm(ad, aup_ref[...]))
  g_out[0] = _mm(_sigmoid(gd), gup_ref[...])
  kk = k * kk_ref[...]
  bd64 = _block_ones(RW_WIDTH, RW_HEAD_DIM)
  kkn = kk * lax.rsqrt(_seg_sum(kk * kk, bd64) + 1e-12)
  r_out[0] = r
  k_out[0] = k * (1.0 + (a - 1.0) * ka_ref[...])
  v_out[0] = v
  kkn_out[0] = kkn
  b_out[0] = kkn * a

  c_q = _dot(xn, win_ref[:, COL_CQ:COL_CQ + MLA_Q_RANK])
  c_kv = _dot(xn, win_ref[:, COL_CKV:COL_CKV + MLA_KV_RANK])
  k_rope = _dot(xn, win_ref[:, COL_ROPE:COL_ROPE + LANES])
  cqn = (c_q * _rms_scale(c_q, MLA_Q_RANK) * cqg_ref[...]).astype(BF16)
  ckvn = (c_kv * _rms_scale(c_kv, MLA_KV_RANK) * ckvg_ref[...]).astype(BF16)
  bd128 = _block_ones(MLA_HEADS * MLA_SLOT, MLA_SLOT)
  cos = cos_ref[0]
  sin = sin_ref[0]
  q = _head_rms(_dot(cqn, wuq_ref[...]), qg_ref[...], bd128)
  kf = _dot(ckvn, wuk_ref[...]) + jnp.concatenate([k_rope] * MLA_HEADS, axis=-1)
  kf = _head_rms(kf, kg_ref[...], bd128)
  mq_out[0] = (_rotary(q, cos, sin) * (MLA_QK_DIM ** -0.5)).astype(BF16)
  mk_out[0] = _rotary(kf, cos, sin).astype(BF16)
  mv_out[0] = _dot(ckvn, wuv_ref[...]).astype(BF16)

  sq_out[0] = (_dot(xn, win_ref[:, COL_SBQ:COL_SBQ + SB_WIDTH]) * (SB_HEAD_DIM ** -0.5)).astype(BF16)
  sk_out[0] = _dot(xn, win_ref[:, COL_SBK:COL_SBK + SB_WIDTH]).astype(BF16)
  sv_out[0] = _dot(xn, win_ref[:, COL_SBV:COL_SBV + SB_WIDTH]).astype(BF16)


def _inproj(x, cos, sin, lw, tm):
  b, t, _ = x.shape
  row = lambda i, j: (i, j, 0)
  const = lambda i, j: (0, 0)
  full = lambda a: pl.BlockSpec(a.shape, const)
  weights = [lw[n] for n in ("ng", "win", "mu", "wup", "w0", "aup", "a0", "gup", "kk", "ka",
                             "cqg", "ckvg", "wuq", "wuk", "wuv", "qg", "kg")]
  rw = jax.ShapeDtypeStruct((b, t, RW_WIDTH), F32)
  att4 = jax.ShapeDtypeStruct((b, t, MLA_HEADS * MLA_SLOT), BF16)
  att2 = jax.ShapeDtypeStruct((b, t, SB_WIDTH), BF16)
  out_shape = (rw,) * 7 + (att4, att4, att2, att2, att2, att2)
  out_specs = tuple(pl.BlockSpec((1, tm, s.shape[-1]), row) for s in out_shape)
  return pl.pallas_call(
      _inproj_kernel,
      out_shape=out_shape,
      grid=(b, t // tm),
      in_specs=[pl.BlockSpec((1, tm, D_MODEL), row),
                pl.BlockSpec((1, tm, MLA_SLOT), row),
                pl.BlockSpec((1, tm, MLA_SLOT), row)] + [full(w) for w in weights],
      out_specs=out_specs,
      scratch_shapes=[pltpu.VMEM((1, RW_COLS), F32)],
      compiler_params=pltpu.CompilerParams(
          dimension_semantics=("arbitrary", "arbitrary"), vmem_limit_bytes=VMEM_LIMIT_BYTES),
      name="inproj",
  )(x, cos, sin, *weights)


RW_PASSES_SCORE = 1
RW_PASSES_INV = 1
RW_PASSES_MID = 1
RW_PASSES_STATE = 1


def _rwkv_phase1(items, p_inv, p_score, p_mid):
  c = CHUNK
  n_items = len(items)
  lane = _iota((c, LANES), 1)
  lo = lane < RW_HEAD_DIM
  stack = lambda m: jnp.concatenate([jnp.where(lo, m, 0.0), jnp.where(lo, 0.0, m)], axis=0)
  r2 = _iota((2 * c, 2 * c), 0)
  c2 = _iota((2 * c, 2 * c), 1)
  eye = jnp.where(r2 == c2, 1.0, 0.0).astype(F32)
  r4 = _iota((4 * c, 4 * c), 0)
  c4 = _iota((4 * c, 4 * c), 1)
  keep = ((r4 & (c - 1)) + jnp.where(r4 >= 2 * c, 1, 0)) > (c4 & (c - 1))
  mmi = functools.partial(_mm, pa=p_inv, pb=p_inv)
  mmm = functools.partial(_mm, pa=p_mid, pb=p_mid)

  st = []
  for at, rt, bt, kt, v, p_end in items:
    a_st, r_st, b_st, k_st, v_st = stack(at), stack(rt), stack(bt), stack(kt), stack(v)
    rhs = jnp.concatenate([b_st, k_st], axis=0)
    sc = _mm(jnp.concatenate([a_st, r_st], axis=0), rhs, _NT, pa=p_score, pb=p_score)
    sc = jnp.where(keep, sc, 0.0)
    st.append(dict(a_st=a_st, r_st=r_st, v_st=v_st, rhs=rhs, p_end=p_end,
                   n=sc[0:2 * c, 0:2 * c], a_ak=sc[0:2 * c, 2 * c:4 * c], r_bk=sc[2 * c:4 * c, :]))
  for s in st:
    s["nd"] = jnp.where((r2 >> 3) == (c2 >> 3), s["n"], 0.0)
  for s in st:
    s["n2"] = mmi(s["nd"], s["nd"])
  for s in st:
    s["n4"] = mmi(s["n2"], s["n2"])
    s["p1"] = eye + s["nd"] + s["n2"] + mmi(s["nd"], s["n2"])
  for s in st:
    s["inv"] = s["p1"] + mmi(s["p1"], s["n4"])
  shift = 3
  while (1 << shift) < c:
    sel = ((r2 >> (shift + 1)) == (c2 >> (shift + 1))) & ((r2 >> shift) != (c2 >> shift))
    for s in st:
      s["tmp"] = mmi(s["inv"], jnp.where(sel, s["n"], 0.0))
    for s in st:
      s["inv"] = s["inv"] + mmi(s["tmp"], s["inv"])
    shift += 1
  for s in st:
    s["av"] = mmm(s["a_ak"], s["v_st"])
  for s in st:
    s["wu"] = mmm(s["inv"], jnp.concatenate([s["a_st"], s["av"]], axis=1))
  out = []
  for s in st:
    big = jnp.concatenate([s["wu"], jnp.concatenate([jnp.zeros_like(s["v_st"]), s["v_st"]], axis=1)], axis=0)
    rwy = mmm(s["r_bk"], big)
    ms = mmm(big, s["rhs"], _TN)
    m = (eye + ms[0:2 * c]) * s["p_end"]
    s1 = ms[2 * c:4 * c] * s["p_end"]
    out.append((s["r_st"] + rwy[:, 0:LANES], rwy[:, LANES:2 * LANES], m, s1))
  assert len(out) == n_items
  return out


def _rwkv_kernel(r_ref, lw_ref, k_ref, v_ref, kkn_ref, b_ref, g_ref, rk_ref, lng_ref, lnb_ref,
                 o_ref, s_ref, y_ref):
  j = pl.program_id(1)
  pairs = RW_WIDTH // LANES

  @pl.when(j == 0)
  def _():
    s_ref[...] = jnp.zeros_like(s_ref)

  tc = r_ref.shape[1]
  rr = _iota((CHUNK, CHUNK), 0)
  cc = _iota((CHUNK, CHUNK), 1)
  tri = jnp.where(rr >= cc, 1.0, 0.0).astype(BF16)
  items = []
  for ci in range(tc // CHUNK):
    rows = slice(ci * CHUNK, (ci + 1) * CHUNK)
    lw = lw_ref[0, rows, :]
    lp = _mm(tri, lw, pa=1, pb=3)
    e_in = jnp.exp(lp)
    e_ex = jnp.exp(lp - lw)
    e_neg = jnp.exp(-lp)
    at = -kkn_ref[0, rows, :] * e_ex
    rt = r_ref[0, rows, :] * e_in
    bt = b_ref[0, rows, :] * e_neg
    kt = k_ref[0, rows, :] * e_neg
    v = v_ref[0, rows, :]
    p_end = e_in[CHUNK - 1:CHUNK, :]
    for p in range(pairs):
      ln = slice(p * LANES, (p + 1) * LANES)
      items.append((at[:, ln], rt[:, ln], bt[:, ln], kt[:, ln], v[:, ln], p_end[:, ln]))
  pre = _rwkv_phase1(items, RW_PASSES_INV, RW_PASSES_SCORE, RW_PASSES_MID)

  for ci in range(tc // CHUNK):
    rows = slice(ci * CHUNK, (ci + 1) * CHUNK)
    for p in range(pairs):
      rw, y0, m, s1 = pre[ci * pairs + p]
      s = s_ref[p]
      y_st = _mm(rw, s, _NT, pa=RW_PASSES_STATE, pb=RW_PASSES_STATE) + y0
      y_ref[rows, p * LANES:(p + 1) * LANES] = y_st[0:CHUNK] + y_st[CHUNK:2 * CHUNK]
      s_ref[p] = _mm(s, m, pa=RW_PASSES_STATE, pb=RW_PASSES_STATE) + s1

  y = y_ref[...]
  bd = _block_ones(RW_WIDTH, RW_HEAD_DIM)
  inv_d = 1.0 / RW_HEAD_DIM
  mean = _seg_sum(y, bd) * inv_d
  d = y - mean
  var = _seg_sum(d * d, bd) * inv_d
  yn = d * lax.rsqrt(var + RW_GN_EPS) * lng_ref[...] + lnb_ref[...]
  bonus = _seg_sum(r_ref[0] * k_ref[0] * rk_ref[...], bd) * v_ref[0]
  o_ref[0] = ((yn + bonus) * g_ref[0]).astype(BF16)


def _rwkv(r, lw, k, v, kkn, bvec, g, rk, lng, lnb, tc):
  b, t, _ = r.shape
  row = lambda i, j: (i, j, 0)
  const = lambda i, j: (0, 0)
  seq = pl.BlockSpec((1, tc, RW_WIDTH), row)
  vec = pl.BlockSpec((1, RW_WIDTH), const)
  return pl.pallas_call(
      _rwkv_kernel,
      out_shape=jax.ShapeDtypeStruct((b, t, RW_WIDTH), BF16),
      grid=(b, t // tc),
      in_specs=[seq] * 7 + [vec] * 3,
      out_specs=seq,
      scratch_shapes=[pltpu.VMEM((RW_WIDTH // LANES, LANES, LANES), F32),
                      pltpu.VMEM((tc, RW_WIDTH), F32)],
      compiler_params=pltpu.CompilerParams(
          dimension_semantics=("arbitrary", "arbitrary"), vmem_limit_bytes=VMEM_LIMIT_BYTES),
      name="rwkv",
  )(r, lw, k, v, kkn, bvec, g, rk, lng, lnb)


def _mla_kernel(q_ref, k_ref, v_ref, o_ref, *, tq, tk):
  i = pl.program_id(1)
  pairs = MLA_HEADS // 2
  lo = _iota((tq, LANES), 1) < MLA_V_DIM
  vlo = _iota((tk, LANES), 1) < MLA_V_DIM

  def block(jb, carry, masked):
    ms, ls, accs = carry
    ks = pl.multiple_of(jb * tk, tk)
    if masked:
      qc = (i * tq + _iota((tq, tk), 0)) >> CHUNK_SHIFT
      kc = (jb * tk + _iota((tq, tk), 1)) >> CHUNK_SHIFT
      visible = kc <= qc
    new_m, new_l, new_acc = [], [], []
    for p in range(pairs):
      vb = v_ref[0, pl.ds(ks, tk), p * LANES:(p + 1) * LANES]
      zero = jnp.zeros_like(vb)
      vcat = jnp.concatenate([jnp.where(vlo, vb, zero), jnp.where(vlo, zero, vb)], axis=0)
      probs, alphas = [], []
      for h in (2 * p, 2 * p + 1):
        q = q_ref[0, :, h * MLA_SLOT:(h + 1) * MLA_SLOT]
        kb = k_ref[0, pl.ds(ks, tk), h * MLA_SLOT:(h + 1) * MLA_SLOT]
        s = _dot(q, kb, _NT)
        if masked:
          s = jnp.where(visible, s, NEG_INF)
        m_new = jnp.maximum(ms[h], jnp.max(s, axis=-1, keepdims=True))
        alpha = jnp.exp(ms[h] - m_new)
        pr = jnp.exp(s - m_new)
        new_m.append(m_new)
        new_l.append(alpha * ls[h] + jnp.sum(pr, axis=-1, keepdims=True))
        probs.append(pr.astype(BF16))
        alphas.append(alpha)
      scale = jnp.where(lo, alphas[0], alphas[1])
      new_acc.append(scale * accs[p] + _dot(jnp.concatenate(probs, axis=1), vcat))
    return tuple(new_m), tuple(new_l), tuple(new_acc)

  init = (tuple(jnp.full((tq, 1), NEG_INF, F32) for _ in range(MLA_HEADS)),
          tuple(jnp.zeros((tq, 1), F32) for _ in range(MLA_HEADS)),
          tuple(jnp.zeros((tq, LANES), F32) for _ in range(pairs)))
  nfull = i * (tq // tk)
  carry = lax.fori_loop(0, nfull, functools.partial(block, masked=False), init)
  for d in range(tq // tk):
    carry = block(nfull + d, carry, True)
  _, ls, accs = carry
  for p in range(pairs):
    o_ref[0, :, p * LANES:(p + 1) * LANES] = (accs[p] / jnp.where(lo, ls[2 * p], ls[2 * p + 1])).astype(BF16)


def _mla(q, k, v, tq, tk):
  b, t, _ = q.shape
  return pl.pallas_call(
      functools.partial(_mla_kernel, tq=tq, tk=tk),
      out_shape=jax.ShapeDtypeStruct((b, t, MLA_WIDTH), BF16),
      grid=(b, t // tq),
      in_specs=[pl.BlockSpec((1, tq, MLA_HEADS * MLA_SLOT), lambda bi, i: (bi, i, 0)),
                pl.BlockSpec((1, t, MLA_HEADS * MLA_SLOT), lambda bi, i: (bi, 0, 0)),
                pl.BlockSpec((1, t, MLA_WIDTH), lambda bi, i: (bi, 0, 0))],
      out_specs=pl.BlockSpec((1, tq, MLA_WIDTH), lambda bi, i: (bi, i, 0)),
      compiler_params=pltpu.CompilerParams(
          dimension_semantics=("arbitrary", "arbitrary"), vmem_limit_bytes=VMEM_LIMIT_BYTES),
      name="mla_attention",
  )(q, k, v)


SB_STATIC_BLOCKS = 3


def _sb_kernel(q_ref, k_ref, v_ref, o_ref, *, tq, tk):
  i = pl.program_id(1)
  pairs = SB_HEADS // 2
  lo = _iota((tq, LANES), 1) < SB_HEAD_DIM
  vlo = _iota((tk, LANES), 1) < SB_HEAD_DIM
  rr = _iota((tk, tk + LANES), 0)
  cc = _iota((tk, tk + LANES), 1)
  tri = jnp.where((rr >= cc) | (cc >= tk), 1.0, 0.0).astype(BF16)

  def sweep(jbs, runs, accs, guard):
    qpos = i * tq + _iota((tq, tk), 0)
    stricts, kbs, vcats = [], [], []
    for jb in jbs:
      ks = pl.multiple_of(jnp.maximum(jb, 0) * tk, tk)
      kpos = jb * tk + _iota((tq, tk), 1)
      strict = kpos < qpos
      stricts.append(strict & (kpos >= 0) if guard else strict)
      kbs.append([k_ref[0, pl.ds(ks, tk), p * LANES:(p + 1) * LANES] for p in range(pairs)])
      vc = []
      for p in range(pairs):
        vb = v_ref[0, pl.ds(ks, tk), p * LANES:(p + 1) * LANES]
        vzero = jnp.zeros_like(vb)
        vc.append(jnp.concatenate([jnp.where(vlo, vb, vzero), jnp.where(vlo, vzero, vb)], axis=0))
      vcats.append(vc)
    qs = []
    for p in range(pairs):
      qp = q_ref[0, :, p * LANES:(p + 1) * LANES]
      qzero = jnp.zeros_like(qp)
      qs += [jnp.where(lo, qp, qzero), jnp.where(lo, qzero, qp)]
    heads = range(SB_HEADS)
    z = [[_dot(qs[h], kbs[d][h // 2], _NT) for h in heads] for d in range(len(jbs))]
    ls = [[jnp.where(stricts[d], -_softplus(z[d][h]), 0.0) for h in heads] for d in range(len(jbs))]
    cs = [[_mm(ls[d][h], tri, pa=2, pb=1) for h in heads] for d in range(len(jbs))]
    runs = list(runs)
    weights = [[] for _ in range(pairs)]
    for d in range(len(jbs)):
      for h in heads:
        logw = z[d][h] + cs[d][h][:, 0:tk] + jnp.concatenate([runs[h]] * (tk // LANES), axis=-1)
        weights[h // 2].append(jnp.where(stricts[d], jnp.exp(logw), 0.0).astype(BF16))
        runs[h] = runs[h] + cs[d][h][:, tk:tk + LANES]
    new_accs = []
    for p in range(pairs):
      vall = jnp.concatenate([vcats[d][p] for d in range(len(jbs))], axis=0)
      new_accs.append(accs[p] + _dot(jnp.concatenate(weights[p], axis=1), vall))
    return tuple(runs), tuple(new_accs)

  zr = jnp.zeros((tq, LANES), F32)
  start = (i + 1) * (tq // tk) - 1
  runs, accs = sweep([start - d for d in range(SB_STATIC_BLOCKS)], (zr,) * SB_HEADS, (zr,) * pairs, True)

  def cond(state):
    jb, runs, _ = state
    top = functools.reduce(jnp.maximum, runs)
    return (jb >= 0) & (jnp.max(top) > SB_LOG_FLOOR)

  def body(state):
    jb, runs, accs = state
    runs, accs = sweep([jb], runs, accs, False)
    return jb - 1, runs, accs

  _, _, accs = lax.while_loop(cond, body, (start - SB_STATIC_BLOCKS, runs, accs))
  for p in range(pairs):
    o_ref[0, :, p * LANES:(p + 1) * LANES] = accs[p].astype(BF16)


def _sb(q, k, v, tq, tk):
  b, t, _ = q.shape
  return pl.pallas_call(
      functools.partial(_sb_kernel, tq=tq, tk=tk),
      out_shape=jax.ShapeDtypeStruct((b, t, SB_WIDTH), BF16),
      grid=(b, t // tq),
      in_specs=[pl.BlockSpec((1, tq, SB_WIDTH), lambda bi, i: (bi, i, 0)),
                pl.BlockSpec((1, t, SB_WIDTH), lambda bi, i: (bi, 0, 0)),
                pl.BlockSpec((1, t, SB_WIDTH), lambda bi, i: (bi, 0, 0))],
      out_specs=pl.BlockSpec((1, tq, SB_WIDTH), lambda bi, i: (bi, i, 0)),
      compiler_params=pltpu.CompilerParams(
          dimension_semantics=("arbitrary", "arbitrary"), vmem_limit_bytes=VMEM_LIMIT_BYTES),
      name="sb_attention",
  )(q, k, v)


def _outffn_kernel(x_ref, yrw_ref, ymla_ref, ysb_ref, wo_ref, ng_ref, wg_ref, wu_ref, wd_ref,
                   o_ref, x1_ref, hn_ref, acc_ref):
  j = pl.program_id(1)

  @pl.when(j == 0)
  def _():
    mix = (_dot(yrw_ref[...], wo_ref[0:RW_WIDTH, :])
           + _dot(ymla_ref[...], wo_ref[RW_WIDTH:RW_WIDTH + MLA_WIDTH, :])
           + _dot(ysb_ref[...], wo_ref[RW_WIDTH + MLA_WIDTH:, :]))
    x1 = x_ref[...] + mix
    x1_ref[...] = x1
    hn_ref[...] = (x1 * _rms_scale(x1, D_MODEL) * ng_ref[...]).astype(BF16)
    acc_ref[...] = jnp.zeros_like(acc_ref)

  hn = hn_ref[...]
  gate = _dot(hn, wg_ref[...])
  up = _dot(hn, wu_ref[...])
  act = (gate * _sigmoid(gate) * up).astype(BF16)
  acc_ref[...] += _dot(act, wd_ref[...])

  @pl.when(j == pl.num_programs(1) - 1)
  def _():
    o_ref[...] = x1_ref[...] + acc_ref[...]


def _outffn(x, yrw, ymla, ysb, lw, tm, th):
  n = x.shape[0]
  row = lambda i, j: (i, 0)
  const = lambda i, j: (0, 0)
  return pl.pallas_call(
      _outffn_kernel,
      out_shape=jax.ShapeDtypeStruct((n, D_MODEL), F32),
      grid=(n // tm, FFN_HIDDEN // th),
      in_specs=[pl.BlockSpec((tm, D_MODEL), row),
                pl.BlockSpec((tm, RW_WIDTH), row),
                pl.BlockSpec((tm, MLA_WIDTH), row),
                pl.BlockSpec((tm, SB_WIDTH), row),
                pl.BlockSpec((D_MODEL, D_MODEL), const),
                pl.BlockSpec((1, D_MODEL), const),
                pl.BlockSpec((D_MODEL, th), lambda i, j: (0, j)),
                pl.BlockSpec((D_MODEL, th), lambda i, j: (0, j)),
                pl.BlockSpec((th, D_MODEL), lambda i, j: (j, 0))],
      out_specs=pl.BlockSpec((tm, D_MODEL), row),
      scratch_shapes=[pltpu.VMEM((tm, D_MODEL), F32),
                      pltpu.VMEM((tm, D_MODEL), BF16),
                      pltpu.VMEM((tm, D_MODEL), F32)],
      compiler_params=pltpu.CompilerParams(
          dimension_semantics=("arbitrary", "arbitrary"), vmem_limit_bytes=VMEM_LIMIT_BYTES),
      name="outproj_ffn",
  )(x, yrw, ymla, ysb, lw["wo"], lw["fg"], lw["wg"], lw["wu"], lw["wd"])


def _pad_heads(w, heads, dim, slot):
  lead = w.shape[:-1]
  w = w.reshape(lead + (heads, dim))
  w = jnp.pad(w, [(0, 0)] * len(lead) + [(0, 0), (0, slot - dim)])
  return w.reshape(lead + (heads * slot,))


def _layer_weights(l, attn_norm_g, w_in, rw_shift_mu, rw_w_up, rw_w0, rw_a_up, rw_a0, rw_g_up, rw_k_k,
                   rw_k_a, rw_r_k, rw_ln_g, rw_ln_b, mla_cq_norm_g, mla_ckv_norm_g, mla_w_uq, mla_w_ukv,
                   mla_q_norm_g, mla_k_norm_g, w_o, ffn_norm_g, ffn_w_gate, ffn_w_up, ffn_w_down):
  vec = lambda a: a[l].reshape(1, -1)
  win = w_in[l]
  rest = RW_COLS
  c_q = win[:, rest:rest + MLA_Q_RANK]
  c_kv = win[:, rest + MLA_Q_RANK:rest + MLA_Q_RANK + MLA_KV_RANK]
  o = rest + MLA_Q_RANK + MLA_KV_RANK
  k_rope = win[:, o:o + MLA_ROPE_DIM]
  sb = win[:, o + MLA_ROPE_DIM:]
  rope_slot = jnp.pad(k_rope, ((0, 0), (MLA_NOPE_DIM, LANES - MLA_NOPE_DIM - MLA_ROPE_DIM)))
  win_p = jnp.concatenate([win[:, :RW_COLS], c_q, c_kv, sb, rope_slot], axis=1).astype(BF16)
  ukv = mla_w_ukv[l].reshape(MLA_KV_RANK, MLA_HEADS, MLA_NOPE_DIM + MLA_V_DIM)
  wuk = _pad_heads(ukv[:, :, :MLA_NOPE_DIM].reshape(MLA_KV_RANK, -1), MLA_HEADS, MLA_NOPE_DIM, MLA_SLOT)
  wuv = ukv[:, :, MLA_NOPE_DIM:].reshape(MLA_KV_RANK, MLA_HEADS * MLA_V_DIM)
  head_gain = lambda g: jnp.tile(jnp.pad(g[l], (0, MLA_SLOT - MLA_QK_DIM)), MLA_HEADS).reshape(1, -1)
  return dict(
      ng=vec(attn_norm_g), win=win_p, mu=vec(rw_shift_mu), wup=rw_w_up[l], w0=vec(rw_w0),
      aup=rw_a_up[l], a0=vec(rw_a0), gup=rw_g_up[l], kk=vec(rw_k_k), ka=vec(rw_k_a),
      rk=vec(rw_r_k), lng=vec(rw_ln_g), lnb=vec(rw_ln_b),
      cqg=vec(mla_cq_norm_g), ckvg=vec(mla_ckv_norm_g),
      wuq=_pad_heads(mla_w_uq[l], MLA_HEADS, MLA_QK_DIM, MLA_SLOT).astype(BF16),
      wuk=wuk.astype(BF16), wuv=wuv.astype(BF16),
      qg=head_gain(mla_q_norm_g), kg=head_gain(mla_k_norm_g),
      wo=w_o[l].astype(BF16), fg=vec(ffn_norm_g),
      wg=ffn_w_gate[l].astype(BF16), wu=ffn_w_up[l].astype(BF16), wd=ffn_w_down[l].astype(BF16))


def kernel(x, positions, attn_norm_g, w_in, rw_shift_mu, rw_w_up, rw_w0, rw_a_up, rw_a0, rw_g_up, rw_k_k, rw_k_a, rw_r_k, rw_ln_g, rw_ln_b, mla_cq_norm_g, mla_ckv_norm_g, mla_w_uq, mla_w_ukv, mla_q_norm_g, mla_k_norm_g, w_o, ffn_norm_g, ffn_w_gate, ffn_w_up, ffn_w_down):
  b, t, d = x.shape
  depth = w_in.shape[0]
  params = (attn_norm_g, w_in, rw_shift_mu, rw_w_up, rw_w0, rw_a_up, rw_a0, rw_g_up, rw_k_k, rw_k_a, rw_r_k,
            rw_ln_g, rw_ln_b, mla_cq_norm_g, mla_ckv_norm_g, mla_w_uq, mla_w_ukv, mla_q_norm_g, mla_k_norm_g,
            w_o, ffn_norm_g, ffn_w_gate, ffn_w_up, ffn_w_down)
  tm = min(256, t)
  cos, sin = _rope_tables(positions, tm)
  for l in range(depth):
    lw = _layer_weights(l, *params)
    (r, lwd, k, v, kkn, bvec, g, mq, mk, mv, sq, sk, sv) = _inproj(x, cos, sin, lw, tm)
    y_rw = _rwkv(r, lwd, k, v, kkn, bvec, g, lw["rk"], lw["lng"], lw["lnb"], tc=min(256, t))
    y_mla = _mla(mq, mk, mv, tq=min(256, t), tk=min(256, t))
    y_sb = _sb(sq, sk, sv, tq=LANES, tk=LANES)
    x = _outffn(x.reshape(b * t, d), y_rw.reshape(b * t, -1), y_mla.reshape(b * t, -1),
                y_sb.reshape(b * t, -1), lw, tm=512, th=FFN_HIDDEN // 2).reshape(b, t, d)
  return x
```

```python
import functools

import jax
import jax.numpy as jnp
import numpy as np
from jax import lax
from jax.experimental import pallas as pl
from jax.experimental.pallas import tpu as pltpu

F32 = jnp.float32
BF16 = jnp.bfloat16

LANES = 128
VMEM_LIMIT_BYTES = 56 * 1024 * 1024

D_MODEL = 1024
CHUNK = 64
CHUNK_SHIFT = 6
RW_HEADS = 8
RW_HEAD_DIM = 64
RW_WIDTH = RW_HEADS * RW_HEAD_DIM
RW_DECAY_LORA = 64
RW_AAA_LORA = 64
RW_GATE_LORA = 128
RW_COLS = 3 * RW_WIDTH + RW_DECAY_LORA + RW_AAA_LORA + RW_GATE_LORA
RW_GN_EPS = 64e-5
MLA_HEADS = 4
MLA_NOPE_DIM = 64
MLA_ROPE_DIM = 32
MLA_V_DIM = 64
MLA_QK_DIM = MLA_NOPE_DIM + MLA_ROPE_DIM
MLA_Q_RANK = 256
MLA_KV_RANK = 128
MLA_WIDTH = MLA_HEADS * MLA_V_DIM
MLA_SLOT = LANES
SB_HEADS = 4
SB_HEAD_DIM = 64
SB_WIDTH = SB_HEADS * SB_HEAD_DIM
FFN_HIDDEN = 2816
ROPE_THETA = 10000.0
NORM_EPS = 1e-6
NEG_INF = -1e30
LOG2E = 1.4426950408889634
MLA_BOUND_MARGIN = 1.03
MLA_MAX_SHIFT = 60.0
SB_LOG_FLOOR = -110.0

COL_RW = 0
COL_CQ = RW_COLS
COL_CKV = COL_CQ + MLA_Q_RANK
COL_SBQ = COL_CKV + MLA_KV_RANK
COL_SBK = COL_SBQ + SB_WIDTH
COL_SBV = COL_SBK + SB_WIDTH
COL_ROPE = COL_SBV + SB_WIDTH
IN_COLS_PAD = COL_ROPE + LANES


def _split_bf16(x, parts):
  out = []
  rem = x
  for i in range(parts):
    p = rem.astype(BF16)
    out.append(p)
    if i + 1 < parts:
      rem = rem - p.astype(F32)
  return out


_NN = (((1,), (0,)), ((), ()))
_NT = (((1,), (1,)), ((), ()))
_TN = (((0,), (0,)), ((), ()))


def _dot(a, b, dims=_NN):
  return lax.dot_general(a, b, dims, preferred_element_type=F32)


def _mm(a, b, dims=_NN, pa=2, pb=2):
  as_ = _split_bf16(a, pa) if a.dtype != BF16 else [a]
  bs_ = _split_bf16(b, pb) if b.dtype != BF16 else [b]
  order = max(len(as_), len(bs_))
  acc = None
  for i, ai in enumerate(as_):
    for j, bj in enumerate(bs_):
      if i + j < order:
        t = _dot(ai, bj, dims)
        acc = t if acc is None else acc + t
  return acc


def _sigmoid(x):
  return 1.0 / (1.0 + jnp.exp(-x))


def _softplus(x):
  return jnp.maximum(x, 0.0) + jnp.log(1.0 + jnp.exp(-jnp.abs(x)))


def _iota(shape, dim):
  return lax.broadcasted_iota(jnp.int32, shape, dim)


def _block_ones(block):
  shift = block.bit_length() - 1
  r = _iota((LANES, LANES), 0) >> shift
  c = _iota((LANES, LANES), 1) >> shift
  return jnp.where(r == c, 1.0, 0.0).astype(BF16)


SEG_SUM_PARTS = 2
LORA_PARTS = 1


def _seg_sum(x, bd):
  groups = [_mm(x[:, g:g + LANES], bd, pa=SEG_SUM_PARTS, pb=1) for g in range(0, x.shape[1], LANES)]
  return jnp.concatenate(groups, axis=1)


def _rope_kernel(pos_ref, invf_ref, cos_ref, sin_ref):
  ang = pos_ref[0].astype(F32) * invf_ref[...]
  lane = _iota(ang.shape, 1)
  first_half = lane < MLA_NOPE_DIM + MLA_ROPE_DIM // 2
  cos_ref[0] = jnp.cos(ang)
  sin_ref[0] = jnp.where(first_half, -jnp.sin(ang), jnp.sin(ang))


def _rope_tables(positions, tm):
  b, t = positions.shape
  inv_freq = ROPE_THETA ** (-jnp.arange(0, MLA_ROPE_DIM, 2, dtype=F32) / MLA_ROPE_DIM)
  invf = jnp.zeros((1, MLA_SLOT), F32)
  invf = invf.at[0, MLA_NOPE_DIM:MLA_NOPE_DIM + MLA_ROPE_DIM].set(jnp.concatenate([inv_freq, inv_freq]))
  out = jax.ShapeDtypeStruct((b, t, MLA_SLOT), F32)
  return pl.pallas_call(
      _rope_kernel,
      out_shape=(out, out),
      grid=(b, t // tm),
      in_specs=[pl.BlockSpec((1, tm, 1), lambda i, j: (i, j, 0)),
                pl.BlockSpec((1, MLA_SLOT), lambda i, j: (0, 0))],
      out_specs=(pl.BlockSpec((1, tm, MLA_SLOT), lambda i, j: (i, j, 0)),
                 pl.BlockSpec((1, tm, MLA_SLOT), lambda i, j: (i, j, 0))),
      name="rope_tables",
  )(positions.reshape(b, t, 1), invf)


def _rms_scale(x, width):
  ms = jnp.sum(x * x, axis=-1, keepdims=True) * (1.0 / width)
  return lax.rsqrt(ms + NORM_EPS)


def _rotary(x, cos, sin):
  half = MLA_ROPE_DIM // 2
  outs = []
  for h in range(MLA_HEADS):
    xh = x[:, h * MLA_SLOT:(h + 1) * MLA_SLOT]
    lane = _iota(xh.shape, 1)
    up = pltpu.roll(xh, half, 1)
    down = pltpu.roll(xh, MLA_SLOT - half, 1)
    partner = jnp.where(lane >= MLA_NOPE_DIM + half, up, down)
    outs.append(xh * cos + partner * sin)
  return jnp.concatenate(outs, axis=-1)


def _head_rms(x, gain, bd):
  ms = _seg_sum(x * x, bd) * (1.0 / MLA_QK_DIM)
  return x * lax.rsqrt(ms + NORM_EPS) * gain


def _inproj_kernel(x_ref, cos_ref, sin_ref, ng_ref, win_ref, mu_ref, wup_ref, w0_ref, aup_ref, a0_ref,
                   gup_ref, kk_ref, ka_ref, cqg_ref, ckvg_ref, wuq_ref, wuk_ref, wuv_ref, qg_ref, kg_ref,
                   qb_ref, kb_ref,
                   r_out, lw_out, k_out, v_out, kkn_out, b_out, g_out,
                   mq_out, mk_out, mv_out, sq_out, sk_out, sv_out,
                   carry_ref):
  j = pl.program_id(1)
  x = x_ref[0]
  tm = x.shape[0]
  xn = (x * _rms_scale(x, D_MODEL) * ng_ref[...]).astype(BF16)

  h_rw = _dot(xn, win_ref[:, COL_RW:COL_RW + RW_COLS])

  @pl.when(j == 0)
  def _():
    carry_ref[...] = jnp.zeros_like(carry_ref)

  row = _iota(h_rw.shape, 0)
  prev = jnp.where(row == 0, carry_ref[...], pltpu.roll(h_rw, 1, 0))
  carry_ref[...] = h_rw[tm - 1:tm, :]
  sh = h_rw + mu_ref[...] * (prev - h_rw)
  r = sh[:, 0:RW_WIDTH]
  k = sh[:, RW_WIDTH:2 * RW_WIDTH]
  v = sh[:, 2 * RW_WIDTH:3 * RW_WIDTH]
  o = 3 * RW_WIDTH
  wd = sh[:, o:o + RW_DECAY_LORA]
  ad = sh[:, o + RW_DECAY_LORA:o + RW_DECAY_LORA + RW_AAA_LORA]
  gd = sh[:, o + RW_DECAY_LORA + RW_AAA_LORA:RW_COLS]

  lora = functools.partial(_mm, pa=LORA_PARTS, pb=LORA_PARTS)
  w_raw = -_softplus(-(w0_ref[...] + lora(jnp.tanh(wd), wup_ref[...]))) - 0.5
  lw_out[0] = -jnp.exp(w_raw)
  a = _sigmoid(a0_ref[...] + lora(ad, aup_ref[...]))
  g_out[0] = lora(_sigmoid(gd), gup_ref[...])
  kk = k * kk_ref[...]
  bd64 = _block_ones(RW_HEAD_DIM)
  kkn = kk * lax.rsqrt(_seg_sum(kk * kk, bd64) + 1e-12)
  r_out[0] = r
  k_out[0] = k * (1.0 + (a - 1.0) * ka_ref[...])
  v_out[0] = v
  kkn_out[0] = kkn
  b_out[0] = kkn * a

  c_q = _dot(xn, win_ref[:, COL_CQ:COL_CQ + MLA_Q_RANK])
  c_kv = _dot(xn, win_ref[:, COL_CKV:COL_CKV + MLA_KV_RANK])
  k_rope = _dot(xn, win_ref[:, COL_ROPE:COL_ROPE + LANES])
  cqn = (c_q * _rms_scale(c_q, MLA_Q_RANK) * cqg_ref[...]).astype(BF16)
  ckvn = (c_kv * _rms_scale(c_kv, MLA_KV_RANK) * ckvg_ref[...]).astype(BF16)
  bd128 = _block_ones(MLA_SLOT)
  cos = cos_ref[0]
  sin = sin_ref[0]
  q = _head_rms(_dot(cqn, wuq_ref[...]), qg_ref[...], bd128)
  kf = _dot(ckvn, wuk_ref[...]) + jnp.concatenate([k_rope] * MLA_HEADS, axis=-1)
  kf = _head_rms(kf, kg_ref[...], bd128)
  mq_out[0] = (_rotary(q, cos, sin) * (MLA_QK_DIM ** -0.5 * LOG2E) + qb_ref[...]).astype(BF16)
  mk_out[0] = (_rotary(kf, cos, sin) + kb_ref[...]).astype(BF16)
  mv_out[0] = _dot(ckvn, wuv_ref[...]).astype(BF16)

  sq_out[0] = (_dot(xn, win_ref[:, COL_SBQ:COL_SBQ + SB_WIDTH]) * (SB_HEAD_DIM ** -0.5)).astype(BF16)
  sk_out[0] = _dot(xn, win_ref[:, COL_SBK:COL_SBK + SB_WIDTH]).astype(BF16)
  sv_out[0] = _dot(xn, win_ref[:, COL_SBV:COL_SBV + SB_WIDTH]).astype(BF16)


def _inproj(x, cos, sin, ws, l, tm):
  b, t, _ = x.shape
  row = lambda i, j: (i, j, 0)
  full = lambda a: _layer_spec(a, l)
  weights = [ws[n] for n in ("ng", "win", "mu", "wup", "w0", "aup", "a0", "gup", "kk", "ka",
                             "cqg", "ckvg", "wuq", "wuk", "wuv", "qg", "kg", "qb", "kb")]
  rw = jax.ShapeDtypeStruct((b, t, RW_WIDTH), F32)
  att4 = jax.ShapeDtypeStruct((b, t, MLA_HEADS * MLA_SLOT), BF16)
  att2 = jax.ShapeDtypeStruct((b, t, SB_WIDTH), BF16)
  out_shape = (rw,) * 7 + (att4, att4, att2, att2, att2, att2)
  out_specs = tuple(pl.BlockSpec((1, tm, s.shape[-1]), row) for s in out_shape)
  return pl.pallas_call(
      _inproj_kernel,
      out_shape=out_shape,
      grid=(b, t // tm),
      in_specs=[pl.BlockSpec((1, tm, D_MODEL), row),
                pl.BlockSpec((1, tm, MLA_SLOT), row),
                pl.BlockSpec((1, tm, MLA_SLOT), row)] + [full(w) for w in weights],
      out_specs=out_specs,
      scratch_shapes=[pltpu.VMEM((1, RW_COLS), F32)],
      compiler_params=pltpu.CompilerParams(
          dimension_semantics=("arbitrary", "arbitrary"), vmem_limit_bytes=VMEM_LIMIT_BYTES),
      name="inproj",
  )(x, cos, sin, *weights)


RW_PASSES_SCORE = 1
RW_PASSES_INV = 1
RW_PASSES_MID = 1
RW_PASSES_STATE = 1


def _rwkv_phase1(items, p_inv, p_score, p_mid):
  c = CHUNK
  n_items = len(items)
  lane = _iota((c, LANES), 1)
  lo = lane < RW_HEAD_DIM
  stack = lambda m: jnp.concatenate([jnp.where(lo, m, 0.0), jnp.where(lo, 0.0, m)], axis=0)
  r2 = _iota((2 * c, 2 * c), 0)
  c2 = _iota((2 * c, 2 * c), 1)
  eye = jnp.where(r2 == c2, 1.0, 0.0).astype(F32)
  r4 = _iota((4 * c, 4 * c), 0)
  c4 = _iota((4 * c, 4 * c), 1)
  keep = ((r4 & (c - 1)) + jnp.where(r4 >= 2 * c, 1, 0)) > (c4 & (c - 1))
  mmm = functools.partial(_mm, pa=p_mid, pb=p_mid)

  def both(key_a, key_b, out_key, parts, rows=None):
    for s in st:
      a = s[key_a] if rows is None else rows(s[key_a])
      s[out_key] = _mm(a, s[key_b], pa=parts, pb=parts)

  st = []
  for at, rt, bt, kt, v, p_end in items:
    a_st, r_st, b_st, k_st, v_st = stack(at), stack(rt), stack(bt), stack(kt), stack(v)
    rhs = jnp.concatenate([b_st, k_st], axis=0)
    sc = _mm(jnp.concatenate([a_st, r_st], axis=0), rhs, _NT, pa=p_score, pb=p_score)
    sc = jnp.where(keep, sc, 0.0)
    st.append(dict(a_st=a_st, r_st=r_st, v_st=v_st, rhs=rhs, p_end=p_end,
                   n=sc[0:2 * c, 0:2 * c], a_ak=sc[0:2 * c, 2 * c:4 * c], r_bk=sc[2 * c:4 * c, :]))
  for s in st:
    s["nd"] = jnp.where((r2 >> 3) == (c2 >> 3), s["n"], 0.0)
  both("nd", "nd", "n2", p_inv)
  both("n2", "n2", "n4", p_inv)
  both("nd", "n2", "n3", p_inv)
  for s in st:
    s["p1"] = eye + s["nd"] + s["n2"] + s["n3"]
  both("p1", "n4", "t", p_inv)
  for s in st:
    s["inv"] = s["p1"] + s["t"]
  shift = 3
  while (1 << shift) < c:
    bs = 1 << shift
    groups = range(2 * c // (2 * bs))
    lower = lambda m, bs=bs, groups=groups: jnp.concatenate(
        [m[g * 2 * bs + bs:(g + 1) * 2 * bs] for g in groups], axis=0)
    sel = ((r2 >> (shift + 1)) == (c2 >> (shift + 1))) & ((r2 >> shift) != (c2 >> shift))
    for s in st:
      s["off"] = jnp.where(sel, s["n"], 0.0)
    both("inv", "off", "t", p_inv, rows=lower)
    both("t", "inv", "t", p_inv)
    for s in st:
      inv, t = s["inv"], s["t"]
      pieces = []
      for g in groups:
        pieces += [inv[g * 2 * bs:g * 2 * bs + bs], inv[g * 2 * bs + bs:(g + 1) * 2 * bs] + t[g * bs:(g + 1) * bs]]
      s["inv"] = jnp.concatenate(pieces, axis=0)
    shift += 1
  both("a_ak", "v_st", "av", p_mid)
  for s in st:
    s["wu"] = mmm(s["inv"], jnp.concatenate([s["a_st"], s["av"]], axis=1))
  out = []
  for s in st:
    wt, ut = s["wu"][:, 0:LANES], s["wu"][:, LANES:2 * LANES]
    r_b, r_k = s["r_bk"][:, 0:2 * c], s["r_bk"][:, 2 * c:4 * c]
    rwy = mmm(r_b, s["wu"])
    y0 = rwy[:, LANES:2 * LANES] + mmm(r_k, s["v_st"])
    m = (eye + mmm(wt, s["rhs"][0:2 * c], _TN)) * s["p_end"]
    s1 = mmm(jnp.concatenate([ut, s["v_st"]], axis=0), s["rhs"], _TN) * s["p_end"]
    out.append((s["r_st"] + rwy[:, 0:LANES], y0, m, s1))
  assert len(out) == n_items
  return out


def _rwkv_kernel(r_ref, lw_ref, k_ref, v_ref, kkn_ref, b_ref, g_ref, rk_ref, lng_ref, lnb_ref,
                 o_ref, s_ref, y_ref):
  j = pl.program_id(1)
  pairs = RW_WIDTH // LANES

  @pl.when(j == 0)
  def _():
    s_ref[...] = jnp.zeros_like(s_ref)

  tc = r_ref.shape[1]
  rr = _iota((CHUNK, CHUNK), 0)
  cc = _iota((CHUNK, CHUNK), 1)
  tri = jnp.where(rr >= cc, 1.0, 0.0).astype(BF16)
  items = []
  for ci in range(tc // CHUNK):
    rows = slice(ci * CHUNK, (ci + 1) * CHUNK)
    lw = lw_ref[0, rows, :]
    lp = _mm(tri, lw, pa=1, pb=3)
    e_in = jnp.exp(lp)
    e_ex = jnp.exp(lp - lw)
    e_neg = jnp.exp(-lp)
    at = -kkn_ref[0, rows, :] * e_ex
    rt = r_ref[0, rows, :] * e_in
    bt = b_ref[0, rows, :] * e_neg
    kt = k_ref[0, rows, :] * e_neg
    v = v_ref[0, rows, :]
    p_end = e_in[CHUNK - 1:CHUNK, :]
    for p in range(pairs):
      ln = slice(p * LANES, (p + 1) * LANES)
      items.append((at[:, ln], rt[:, ln], bt[:, ln], kt[:, ln], v[:, ln], p_end[:, ln]))
  pre = _rwkv_phase1(items, RW_PASSES_INV, RW_PASSES_SCORE, RW_PASSES_MID)

  for ci in range(tc // CHUNK):
    rows = slice(ci * CHUNK, (ci + 1) * CHUNK)
    for p in range(pairs):
      rw, y0, m, s1 = pre[ci * pairs + p]
      s = s_ref[p]
      y_st = _mm(rw, s, _NT, pa=RW_PASSES_STATE, pb=RW_PASSES_STATE) + y0
      y_ref[rows, p * LANES:(p + 1) * LANES] = y_st[0:CHUNK] + y_st[CHUNK:2 * CHUNK]
      s_ref[p] = _mm(s, m, pa=RW_PASSES_STATE, pb=RW_PASSES_STATE) + s1

  y = y_ref[...]
  bd = _block_ones(RW_HEAD_DIM)
  inv_d = 1.0 / RW_HEAD_DIM
  mean = _seg_sum(y, bd) * inv_d
  d = y - mean
  var = _seg_sum(d * d, bd) * inv_d
  yn = d * lax.rsqrt(var + RW_GN_EPS) * lng_ref[...] + lnb_ref[...]
  bonus = _seg_sum(r_ref[0] * k_ref[0] * rk_ref[...], bd) * v_ref[0]
  o_ref[0] = ((yn + bonus) * g_ref[0]).astype(BF16)


def _rwkv(r, lw, k, v, kkn, bvec, g, ws, l, tc):
  b, t, _ = r.shape
  row = lambda i, j: (i, j, 0)
  seq = pl.BlockSpec((1, tc, RW_WIDTH), row)
  rk, lng, lnb = ws["rk"], ws["lng"], ws["lnb"]
  return pl.pallas_call(
      _rwkv_kernel,
      out_shape=jax.ShapeDtypeStruct((b, t, RW_WIDTH), BF16),
      grid=(b, t // tc),
      in_specs=[seq] * 7 + [_layer_spec(a, l) for a in (rk, lng, lnb)],
      out_specs=seq,
      scratch_shapes=[pltpu.VMEM((RW_WIDTH // LANES, LANES, LANES), F32),
                      pltpu.VMEM((tc, RW_WIDTH), F32)],
      compiler_params=pltpu.CompilerParams(
          dimension_semantics=("arbitrary", "arbitrary"), vmem_limit_bytes=VMEM_LIMIT_BYTES),
      name="rwkv",
  )(r, lw, k, v, kkn, bvec, g, rk, lng, lnb)


def _mla_kernel(fixed_ref, q_ref, k_ref, v_ref, o_ref, *, tq, tk):
  i = pl.program_id(1)
  lo = _iota((tq, LANES), 1) < MLA_V_DIM
  vlo = _iota((tk, LANES), 1) < MLA_V_DIM
  nfull = i * (tq // tk)

  def score(jb, h):
    ks = pl.multiple_of(jb * tk, tk)
    q = q_ref[0, :, h * MLA_SLOT:(h + 1) * MLA_SLOT]
    return _dot(q, k_ref[0, pl.ds(ks, tk), h * MLA_SLOT:(h + 1) * MLA_SLOT], _NT)

  def value_tile(jb, h):
    ks = pl.multiple_of(jb * tk, tk)
    vb = v_ref[0, pl.ds(ks, tk), (h // 2) * LANES:(h // 2 + 1) * LANES]
    one = jnp.ones_like(vb)
    return jnp.where(vlo, vb, one) if h % 2 == 0 else jnp.where(vlo, one, vb)

  def visible(jb):
    qc = (i * tq + _iota((tq, tk), 0)) >> CHUNK_SHIFT
    kc = (jb * tk + _iota((tq, tk), 1)) >> CHUNK_SHIFT
    return kc <= qc

  def probs(jb):
    return tuple(jnp.exp2(score(jb, h)).astype(BF16) for h in range(MLA_HEADS))

  def fixed_block(jb, carry, last):
    accs, pr = carry
    if last:
      keep = visible(jb)
      pr = tuple(jnp.where(keep, p.astype(F32), 0.0).astype(BF16) for p in pr)
    accs = tuple(accs[h] + _dot(pr[h], value_tile(jb, h)) for h in range(MLA_HEADS))
    return accs, (pr if last else probs(jb + 1))

  def online_block(jb, carry, masked):
    ms, accs = carry
    new_m, new_acc = [], []
    for h in range(MLA_HEADS):
      s, tile = score(jb, h), value_tile(jb, h)
      if masked:
        s = jnp.where(visible(jb), s, NEG_INF)
      m_new = jnp.maximum(ms[h], jnp.max(s, axis=-1, keepdims=True))
      pr = jnp.exp2(s - m_new)
      new_m.append(m_new)
      new_acc.append(jnp.exp2(ms[h] - m_new) * accs[h] + _dot(pr.astype(BF16), tile))
    return tuple(new_m), tuple(new_acc)

  def finish(accs):
    for p in range(MLA_HEADS // 2):
      a, b = accs[2 * p], accs[2 * p + 1]
      out = jnp.where(lo, a / pltpu.roll(a, MLA_V_DIM, 1), b / pltpu.roll(b, MLA_V_DIM, 1))
      o_ref[0, :, p * LANES:(p + 1) * LANES] = out.astype(BF16)

  zeros = tuple(jnp.zeros((tq, LANES), F32) for _ in range(MLA_HEADS))
  fixed = fixed_ref[0, 0] == 1

  @pl.when(fixed)
  def _():
    carry = lax.fori_loop(0, nfull, functools.partial(fixed_block, last=False), (zeros, probs(0)))
    finish(fixed_block(nfull, carry, True)[0])

  @pl.when(jnp.logical_not(fixed))
  def _():
    carry = (tuple(jnp.full((tq, 1), NEG_INF, F32) for _ in range(MLA_HEADS)), zeros)
    carry = lax.fori_loop(0, nfull, functools.partial(online_block, masked=False), carry)
    finish(online_block(nfull, carry, True)[1])


def _mla(fixed_shift, q, k, v, tq, tk):
  b, t, _ = q.shape
  assert tq == tk, "one masked diagonal block per query tile"
  return pl.pallas_call(
      functools.partial(_mla_kernel, tq=tq, tk=tk),
      out_shape=jax.ShapeDtypeStruct((b, t, MLA_WIDTH), BF16),
      grid=(b, t // tq),
      in_specs=[pl.BlockSpec(memory_space=pltpu.SMEM),
                pl.BlockSpec((1, tq, MLA_HEADS * MLA_SLOT), lambda bi, i: (bi, i, 0)),
                pl.BlockSpec((1, t, MLA_HEADS * MLA_SLOT), lambda bi, i: (bi, 0, 0)),
                pl.BlockSpec((1, t, MLA_WIDTH), lambda bi, i: (bi, 0, 0))],
      out_specs=pl.BlockSpec((1, tq, MLA_WIDTH), lambda bi, i: (bi, i, 0)),
      compiler_params=pltpu.CompilerParams(
          dimension_semantics=("arbitrary", "arbitrary"), vmem_limit_bytes=VMEM_LIMIT_BYTES),
      name="mla_attention",
  )(fixed_shift, q, k, v)


SB_STATIC_BLOCKS = 3


def _sb_kernel(q_ref, k_ref, v_ref, o_ref, *, tq, tk):
  i = pl.program_id(1)
  pairs = SB_HEADS // 2
  lo = _iota((tq, LANES), 1) < SB_HEAD_DIM
  vlo = _iota((tk, LANES), 1) < SB_HEAD_DIM
  rr = _iota((tk, tk + LANES), 0)
  cc = _iota((tk, tk + LANES), 1)
  tri = jnp.where((rr >= cc) | (cc >= tk), 1.0, 0.0).astype(BF16)

  def sweep(jbs, runs, accs, guard):
    qpos = i * tq + _iota((tq, tk), 0)
    stricts, kbs, vcats = [], [], []
    for jb in jbs:
      ks = pl.multiple_of(jnp.maximum(jb, 0) * tk, tk)
      kpos = jb * tk + _iota((tq, tk), 1)
      strict = kpos < qpos
      stricts.append(strict & (kpos >= 0) if guard else strict)
      kbs.append([k_ref[0, pl.ds(ks, tk), p * LANES:(p + 1) * LANES] for p in range(pairs)])
      vc = []
      for p in range(pairs):
        vb = v_ref[0, pl.ds(ks, tk), p * LANES:(p + 1) * LANES]
        vzero = jnp.zeros_like(vb)
        vc.append(jnp.concatenate([jnp.where(vlo, vb, vzero), jnp.where(vlo, vzero, vb)], axis=0))
      vcats.append(vc)
    qs = []
    for p in range(pairs):
      qp = q_ref[0, :, p * LANES:(p + 1) * LANES]
      qzero = jnp.zeros_like(qp)
      qs += [jnp.where(lo, qp, qzero), jnp.where(lo, qzero, qp)]
    heads = range(SB_HEADS)
    z = [[_dot(qs[h], kbs[d][h // 2], _NT) for h in heads] for d in range(len(jbs))]
    ls = [[jnp.where(stricts[d], -_softplus(z[d][h]), 0.0) for h in heads] for d in range(len(jbs))]
    cs = [[_mm(ls[d][h], tri, pa=2, pb=1) for h in heads] for d in range(len(jbs))]
    runs = list(runs)
    weights = [[] for _ in range(pairs)]
    for d in range(len(jbs)):
      for h in heads:
        logw = z[d][h] + cs[d][h][:, 0:tk] + jnp.concatenate([runs[h]] * (tk // LANES), axis=-1)
        weights[h // 2].append(jnp.where(stricts[d], jnp.exp(logw), 0.0).astype(BF16))
        runs[h] = runs[h] + cs[d][h][:, tk:tk + LANES]
    new_accs = []
    for p in range(pairs):
      vall = jnp.concatenate([vcats[d][p] for d in range(len(jbs))], axis=0)
      new_accs.append(accs[p] + _dot(jnp.concatenate(weights[p], axis=1), vall))
    return tuple(runs), tuple(new_accs)

  zr = jnp.zeros((tq, LANES), F32)
  start = (i + 1) * (tq // tk) - 1
  runs, accs = sweep([start - d for d in range(SB_STATIC_BLOCKS)], (zr,) * SB_HEADS, (zr,) * pairs, True)

  def cond(state):
    jb, runs, _ = state
    top = functools.reduce(jnp.maximum, runs)
    return (jb >= 0) & (jnp.max(top) > SB_LOG_FLOOR)

  def body(state):
    jb, runs, accs = state
    runs, accs = sweep([jb], runs, accs, False)
    return jb - 1, runs, accs

  _, _, accs = lax.while_loop(cond, body, (start - SB_STATIC_BLOCKS, runs, accs))
  for p in range(pairs):
    o_ref[0, :, p * LANES:(p + 1) * LANES] = accs[p].astype(BF16)


def _sb(q, k, v, tq, tk):
  b, t, _ = q.shape
  return pl.pallas_call(
      functools.partial(_sb_kernel, tq=tq, tk=tk),
      out_shape=jax.ShapeDtypeStruct((b, t, SB_WIDTH), BF16),
      grid=(b, t // tq),
      in_specs=[pl.BlockSpec((1, tq, SB_WIDTH), lambda bi, i: (bi, i, 0)),
                pl.BlockSpec((1, t, SB_WIDTH), lambda bi, i: (bi, 0, 0)),
                pl.BlockSpec((1, t, SB_WIDTH), lambda bi, i: (bi, 0, 0))],
      out_specs=pl.BlockSpec((1, tq, SB_WIDTH), lambda bi, i: (bi, i, 0)),
      compiler_params=pltpu.CompilerParams(
          dimension_semantics=("arbitrary", "arbitrary"), vmem_limit_bytes=VMEM_LIMIT_BYTES),
      name="sb_attention",
  )(q, k, v)


FFN_CHUNKS = (1024, 1024, 768)
assert sum(FFN_CHUNKS) == FFN_HIDDEN


def _outffn_kernel(x_ref, yrw_ref, ymla_ref, ysb_ref, wo_ref, ng_ref, wg_ref, wu_ref, wd_ref, o_ref):
  mix = (_dot(yrw_ref[...], wo_ref[0:RW_WIDTH, :])
         + _dot(ymla_ref[...], wo_ref[RW_WIDTH:RW_WIDTH + MLA_WIDTH, :])
         + _dot(ysb_ref[...], wo_ref[RW_WIDTH + MLA_WIDTH:, :]))
  x1 = x_ref[...] + mix
  hn = (x1 * _rms_scale(x1, D_MODEL) * ng_ref[...]).astype(BF16)
  acc = x1
  start = 0
  for width in FFN_CHUNKS:
    cols = slice(start, start + width)
    gate = _dot(hn, wg_ref[:, cols])
    up = _dot(hn, wu_ref[:, cols])
    act = (gate * _sigmoid(gate) * up).astype(BF16)
    acc = acc + _dot(act, wd_ref[cols, :])
    start += width
  o_ref[...] = acc


def _outffn(x, yrw, ymla, ysb, ws, l, tm):
  n = x.shape[0]
  row = lambda i: (i, 0)
  resident = lambda a: _layer_spec(a, l, pipeline_mode=pl.Buffered(1))
  weights = [ws[k] for k in ("wo", "fg", "wg", "wu", "wd")]
  return pl.pallas_call(
      _outffn_kernel,
      out_shape=jax.ShapeDtypeStruct((n, D_MODEL), F32),
      grid=(n // tm,),
      in_specs=[pl.BlockSpec((tm, D_MODEL), row),
                pl.BlockSpec((tm, RW_WIDTH), row),
                pl.BlockSpec((tm, MLA_WIDTH), row),
                pl.BlockSpec((tm, SB_WIDTH), row)] + [resident(w) for w in weights],
      out_specs=pl.BlockSpec((tm, D_MODEL), row),
      compiler_params=pltpu.CompilerParams(
          dimension_semantics=("arbitrary",), vmem_limit_bytes=VMEM_LIMIT_BYTES),
      name="outproj_ffn",
  )(x, yrw, ymla, ysb, *weights)


def _pad_heads(w, heads, dim, slot):
  lead = w.shape[:-1]
  w = w.reshape(lead + (heads, dim))
  w = jnp.pad(w, [(0, 0)] * len(lead) + [(0, 0), (0, slot - dim)])
  return w.reshape(lead + (heads * slot,))


def _stack_weights(attn_norm_g, w_in, rw_shift_mu, rw_w_up, rw_w0, rw_a_up, rw_a0, rw_g_up, rw_k_k,
                   rw_k_a, rw_r_k, rw_ln_g, rw_ln_b, mla_cq_norm_g, mla_ckv_norm_g, mla_w_uq, mla_w_ukv,
                   mla_q_norm_g, mla_k_norm_g, w_o, ffn_norm_g, ffn_w_gate, ffn_w_up, ffn_w_down):
  depth = w_in.shape[0]
  vec = lambda a: a.reshape(depth, 1, -1)
  rest = RW_COLS
  c_q = w_in[:, :, rest:rest + MLA_Q_RANK]
  c_kv = w_in[:, :, rest + MLA_Q_RANK:rest + MLA_Q_RANK + MLA_KV_RANK]
  o = rest + MLA_Q_RANK + MLA_KV_RANK
  k_rope = w_in[:, :, o:o + MLA_ROPE_DIM]
  sb = w_in[:, :, o + MLA_ROPE_DIM:]
  rope_slot = jnp.pad(k_rope, ((0, 0), (0, 0), (MLA_NOPE_DIM, LANES - MLA_NOPE_DIM - MLA_ROPE_DIM)))
  win_p = jnp.concatenate([w_in[:, :, :RW_COLS], c_q, c_kv, sb, rope_slot], axis=2).astype(BF16)
  ukv = mla_w_ukv.reshape(depth, MLA_KV_RANK, MLA_HEADS, MLA_NOPE_DIM + MLA_V_DIM)
  wuk = _pad_heads(ukv[..., :MLA_NOPE_DIM].reshape(depth, MLA_KV_RANK, -1), MLA_HEADS, MLA_NOPE_DIM, MLA_SLOT)
  wuv = ukv[..., MLA_NOPE_DIM:].reshape(depth, MLA_KV_RANK, MLA_HEADS * MLA_V_DIM)
  head_gain = lambda g: jnp.tile(jnp.pad(g, ((0, 0), (0, MLA_SLOT - MLA_QK_DIM))), (1, MLA_HEADS)).reshape(depth, 1, -1)
  bound = (MLA_QK_DIM ** 0.5) * jnp.max(jnp.abs(mla_q_norm_g), axis=1) * jnp.max(jnp.abs(mla_k_norm_g), axis=1)
  shift = (MLA_BOUND_MARGIN * LOG2E * bound).reshape(depth, 1, 1)
  fixed_shift = shift <= MLA_MAX_SHIFT
  bias_lane = jnp.tile(jnp.arange(MLA_SLOT) == MLA_QK_DIM, MLA_HEADS).reshape(1, 1, -1)
  qb = jnp.where(bias_lane & fixed_shift, -shift, 0.0).astype(F32)
  kb = jnp.broadcast_to(jnp.where(bias_lane, 1.0, 0.0).astype(F32), qb.shape)
  return dict(
      qb=qb, kb=kb, fixed_shift=fixed_shift.astype(jnp.int32),
      ng=vec(attn_norm_g), win=win_p, mu=vec(rw_shift_mu), wup=rw_w_up, w0=vec(rw_w0),
      aup=rw_a_up, a0=vec(rw_a0), gup=rw_g_up, kk=vec(rw_k_k), ka=vec(rw_k_a),
      rk=vec(rw_r_k), lng=vec(rw_ln_g), lnb=vec(rw_ln_b),
      cqg=vec(mla_cq_norm_g), ckvg=vec(mla_ckv_norm_g),
      wuq=_pad_heads(mla_w_uq, MLA_HEADS, MLA_QK_DIM, MLA_SLOT).astype(BF16),
      wuk=wuk.astype(BF16), wuv=wuv.astype(BF16),
      qg=head_gain(mla_q_norm_g), kg=head_gain(mla_k_norm_g),
      wo=w_o.astype(BF16), fg=vec(ffn_norm_g),
      wg=ffn_w_gate.astype(BF16), wu=ffn_w_up.astype(BF16), wd=ffn_w_down.astype(BF16))


def _layer_spec(a, l, **kwargs):
  zeros = (0,) * (a.ndim - 1)
  return pl.BlockSpec((None,) + a.shape[1:], lambda *_: (l,) + zeros, **kwargs)


def kernel(x, positions, attn_norm_g, w_in, rw_shift_mu, rw_w_up, rw_w0, rw_a_up, rw_a0, rw_g_up, rw_k_k, rw_k_a, rw_r_k, rw_ln_g, rw_ln_b, mla_cq_norm_g, mla_ckv_norm_g, mla_w_uq, mla_w_ukv, mla_q_norm_g, mla_k_norm_g, w_o, ffn_norm_g, ffn_w_gate, ffn_w_up, ffn_w_down):
  b, t, d = x.shape
  depth = w_in.shape[0]
  params = (attn_norm_g, w_in, rw_shift_mu, rw_w_up, rw_w0, rw_a_up, rw_a0, rw_g_up, rw_k_k, rw_k_a, rw_r_k,
            rw_ln_g, rw_ln_b, mla_cq_norm_g, mla_ckv_norm_g, mla_w_uq, mla_w_ukv, mla_q_norm_g, mla_k_norm_g,
            w_o, ffn_norm_g, ffn_w_gate, ffn_w_up, ffn_w_down)
  tm = min(512, t)
  cos, sin = _rope_tables(positions, tm)
  ws = _stack_weights(*params)
  for l in range(depth):
    (r, lwd, k, v, kkn, bvec, g, mq, mk, mv, sq, sk, sv) = _inproj(x, cos, sin, ws, l, tm)
    y_rw = _rwkv(r, lwd, k, v, kkn, bvec, g, ws, l, tc=min(256, t))
    y_mla = _mla(ws["fixed_shift"][l], mq, mk, mv, tq=min(512, t), tk=min(512, t))
    y_sb = _sb(sq, sk, sv, tq=LANES, tk=LANES)
    x = _outffn(x.reshape(b * t, d), y_rw.reshape(b * t, -1), y_mla.reshape(b * t, -1),
                y_sb.reshape(b * t, -1), ws, l, tm=512).reshape(b, t, d)
  return x
```

```python
import functools

import jax
import jax.numpy as jnp
import numpy as np
from jax import lax
from jax.experimental import pallas as pl
from jax.experimental.pallas import tpu as pltpu

F32 = jnp.float32
BF16 = jnp.bfloat16

LANES = 128
VMEM_LIMIT_BYTES = 56 * 1024 * 1024

D_MODEL = 1024
CHUNK = 64
CHUNK_SHIFT = 6
RW_HEADS = 8
RW_HEAD_DIM = 64
RW_WIDTH = RW_HEADS * RW_HEAD_DIM
RW_DECAY_LORA = 64
RW_AAA_LORA = 64
RW_GATE_LORA = 128
RW_COLS = 3 * RW_WIDTH + RW_DECAY_LORA + RW_AAA_LORA + RW_GATE_LORA
RW_GN_EPS = 64e-5
MLA_HEADS = 4
MLA_NOPE_DIM = 64
MLA_ROPE_DIM = 32
MLA_V_DIM = 64
MLA_QK_DIM = MLA_NOPE_DIM + MLA_ROPE_DIM
MLA_Q_RANK = 256
MLA_KV_RANK = 128
MLA_WIDTH = MLA_HEADS * MLA_V_DIM
MLA_SLOT = LANES
SB_HEADS = 4
SB_HEAD_DIM = 64
SB_WIDTH = SB_HEADS * SB_HEAD_DIM
FFN_HIDDEN = 2816
ROPE_THETA = 10000.0
NORM_EPS = 1e-6
NEG_INF = -1e30
MLA_BOUND_MARGIN = 1.03
MLA_MAX_SHIFT = 41.0
SB_LOG_FLOOR = -110.0

COL_RW = 0
COL_CQ = RW_COLS
COL_CKV = COL_CQ + MLA_Q_RANK
COL_SBQ = COL_CKV + MLA_KV_RANK
COL_SBK = COL_SBQ + SB_WIDTH
COL_SBV = COL_SBK + SB_WIDTH
COL_ROPE = COL_SBV + SB_WIDTH
IN_COLS_PAD = COL_ROPE + LANES


def _split_bf16(x, parts):
  out = []
  rem = x
  for i in range(parts):
    p = rem.astype(BF16)
    out.append(p)
    if i + 1 < parts:
      rem = rem - p.astype(F32)
  return out


_NN = (((1,), (0,)), ((), ()))
_NT = (((1,), (1,)), ((), ()))
_TN = (((0,), (0,)), ((), ()))


def _dot(a, b, dims=_NN):
  return lax.dot_general(a, b, dims, preferred_element_type=F32)


def _mm(a, b, dims=_NN, pa=2, pb=2):
  as_ = _split_bf16(a, pa) if a.dtype != BF16 else [a]
  bs_ = _split_bf16(b, pb) if b.dtype != BF16 else [b]
  order = max(len(as_), len(bs_))
  acc = None
  for i, ai in enumerate(as_):
    for j, bj in enumerate(bs_):
      if i + j < order:
        t = _dot(ai, bj, dims)
        acc = t if acc is None else acc + t
  return acc


def _sigmoid(x):
  return 1.0 / (1.0 + jnp.exp(-x))


def _softplus(x):
  return jnp.maximum(x, 0.0) + jnp.log(1.0 + jnp.exp(-jnp.abs(x)))


def _iota(shape, dim):
  return lax.broadcasted_iota(jnp.int32, shape, dim)


def _block_ones(block):
  shift = block.bit_length() - 1
  r = _iota((LANES, LANES), 0) >> shift
  c = _iota((LANES, LANES), 1) >> shift
  return jnp.where(r == c, 1.0, 0.0).astype(BF16)


SEG_SUM_PARTS = 2
LORA_PARTS = 1


def _seg_sum(x, bd):
  groups = [_mm(x[:, g:g + LANES], bd, pa=SEG_SUM_PARTS, pb=1) for g in range(0, x.shape[1], LANES)]
  return jnp.concatenate(groups, axis=1)


def _rope_kernel(pos_ref, invf_ref, cos_ref, sin_ref):
  ang = pos_ref[0].astype(F32) * invf_ref[...]
  lane = _iota(ang.shape, 1)
  first_half = lane < MLA_NOPE_DIM + MLA_ROPE_DIM // 2
  cos_ref[0] = jnp.cos(ang)
  sin_ref[0] = jnp.where(first_half, -jnp.sin(ang), jnp.sin(ang))


def _rope_tables(positions, tm):
  b, t = positions.shape
  inv_freq = ROPE_THETA ** (-jnp.arange(0, MLA_ROPE_DIM, 2, dtype=F32) / MLA_ROPE_DIM)
  invf = jnp.zeros((1, MLA_SLOT), F32)
  invf = invf.at[0, MLA_NOPE_DIM:MLA_NOPE_DIM + MLA_ROPE_DIM].set(jnp.concatenate([inv_freq, inv_freq]))
  out = jax.ShapeDtypeStruct((b, t, MLA_SLOT), F32)
  return pl.pallas_call(
      _rope_kernel,
      out_shape=(out, out),
      grid=(b, t // tm),
      in_specs=[pl.BlockSpec((1, tm, 1), lambda i, j: (i, j, 0)),
                pl.BlockSpec((1, MLA_SLOT), lambda i, j: (0, 0))],
      out_specs=(pl.BlockSpec((1, tm, MLA_SLOT), lambda i, j: (i, j, 0)),
                 pl.BlockSpec((1, tm, MLA_SLOT), lambda i, j: (i, j, 0))),
      name="rope_tables",
  )(positions.reshape(b, t, 1), invf)


INPROJ_SUBTILE = 256


def _rms_scale(x, width):
  ms = jnp.sum(x * x, axis=-1, keepdims=True) * (1.0 / width)
  return lax.rsqrt(ms + NORM_EPS)


def _rotary(x, cos, sin):
  half = MLA_ROPE_DIM // 2
  outs = []
  for h in range(MLA_HEADS):
    xh = x[:, h * MLA_SLOT:(h + 1) * MLA_SLOT]
    lane = _iota(xh.shape, 1)
    up = pltpu.roll(xh, half, 1)
    down = pltpu.roll(xh, MLA_SLOT - half, 1)
    partner = jnp.where(lane >= MLA_NOPE_DIM + half, up, down)
    outs.append(xh * cos + partner * sin)
  return jnp.concatenate(outs, axis=-1)


def _head_rms(x, gain, bd):
  ms = _seg_sum(x * x, bd) * (1.0 / MLA_QK_DIM)
  return x * lax.rsqrt(ms + NORM_EPS) * gain


def _inproj_kernel(x_ref, cos_ref, sin_ref, ng_ref, win_ref, mu_ref, wup_ref, w0_ref, aup_ref, a0_ref,
                   gup_ref, kk_ref, ka_ref, cqg_ref, ckvg_ref, wuq_ref, wuk_ref, wuv_ref, qg_ref, kg_ref,
                   qb_ref, kb_ref,
                   r_out, lw_out, k_out, v_out, kkn_out, b_out, g_out,
                   mq_out, mk_out, mv_out, sq_out, sk_out, sv_out,
                   carry_ref):
  j = pl.program_id(1)

  @pl.when(j == 0)
  def _():
    carry_ref[...] = jnp.zeros_like(carry_ref)

  tm = x_ref.shape[1]
  sub = min(INPROJ_SUBTILE, tm)
  last_row = carry_ref[...]
  for r0 in range(0, tm, sub):
    rows = slice(r0, r0 + sub)
    outs = tuple(o.at[0, rows, :] for o in (r_out, lw_out, k_out, v_out, kkn_out, b_out, g_out,
                                           mq_out, mk_out, mv_out, sq_out, sk_out, sv_out))
    last_row = _inproj_rows(x_ref[0, rows, :], cos_ref[0, rows, :], sin_ref[0, rows, :], last_row,
                            ng_ref, win_ref, mu_ref, wup_ref, w0_ref, aup_ref, a0_ref, gup_ref, kk_ref, ka_ref,
                            cqg_ref, ckvg_ref, wuq_ref, wuk_ref, wuv_ref, qg_ref, kg_ref, qb_ref, kb_ref, outs)
  carry_ref[...] = last_row


def _inproj_rows(x, cos, sin, prev_row, ng_ref, win_ref, mu_ref, wup_ref, w0_ref, aup_ref, a0_ref, gup_ref,
                 kk_ref, ka_ref, cqg_ref, ckvg_ref, wuq_ref, wuk_ref, wuv_ref, qg_ref, kg_ref, qb_ref, kb_ref, outs):
  (r_out, lw_out, k_out, v_out, kkn_out, b_out, g_out, mq_out, mk_out, mv_out, sq_out, sk_out, sv_out) = outs
  n = x.shape[0]
  xn = (x * _rms_scale(x, D_MODEL) * ng_ref[...]).astype(BF16)

  h_rw = _dot(xn, win_ref[:, COL_RW:COL_RW + RW_COLS])
  row = _iota(h_rw.shape, 0)
  prev = jnp.where(row == 0, prev_row, pltpu.roll(h_rw, 1, 0))
  sh = h_rw + mu_ref[...] * (prev - h_rw)
  r = sh[:, 0:RW_WIDTH]
  k = sh[:, RW_WIDTH:2 * RW_WIDTH]
  v = sh[:, 2 * RW_WIDTH:3 * RW_WIDTH]
  o = 3 * RW_WIDTH
  wd = sh[:, o:o + RW_DECAY_LORA]
  ad = sh[:, o + RW_DECAY_LORA:o + RW_DECAY_LORA + RW_AAA_LORA]
  gd = sh[:, o + RW_DECAY_LORA + RW_AAA_LORA:RW_COLS]

  lora = functools.partial(_mm, pa=LORA_PARTS, pb=LORA_PARTS)
  w_raw = -_softplus(-(w0_ref[...] + lora(jnp.tanh(wd), wup_ref[...]))) - 0.5
  lw_out[...] = -jnp.exp(w_raw)
  a = _sigmoid(a0_ref[...] + lora(ad, aup_ref[...]))
  g_out[...] = lora(_sigmoid(gd), gup_ref[...])
  kk = k * kk_ref[...]
  bd64 = _block_ones(RW_HEAD_DIM)
  kkn = kk * lax.rsqrt(_seg_sum(kk * kk, bd64) + 1e-12)
  r_out[...] = r
  k_out[...] = k * (1.0 + (a - 1.0) * ka_ref[...])
  v_out[...] = v
  kkn_out[...] = kkn
  b_out[...] = kkn * a

  c_q = _dot(xn, win_ref[:, COL_CQ:COL_CQ + MLA_Q_RANK])
  c_kv = _dot(xn, win_ref[:, COL_CKV:COL_CKV + MLA_KV_RANK])
  k_rope = _dot(xn, win_ref[:, COL_ROPE:COL_ROPE + LANES])
  cqn = (c_q * _rms_scale(c_q, MLA_Q_RANK) * cqg_ref[...]).astype(BF16)
  ckvn = (c_kv * _rms_scale(c_kv, MLA_KV_RANK) * ckvg_ref[...]).astype(BF16)
  bd128 = _block_ones(MLA_SLOT)
  q = _head_rms(_dot(cqn, wuq_ref[...]), qg_ref[...], bd128)
  kf = _dot(ckvn, wuk_ref[...]) + jnp.concatenate([k_rope] * MLA_HEADS, axis=-1)
  kf = _head_rms(kf, kg_ref[...], bd128)
  mq_out[...] = (_rotary(q, cos, sin) * (MLA_QK_DIM ** -0.5) + qb_ref[...]).astype(BF16)
  mk_out[...] = (_rotary(kf, cos, sin) + kb_ref[...]).astype(BF16)
  mv_out[...] = _dot(ckvn, wuv_ref[...]).astype(BF16)

  sq_out[...] = (_dot(xn, win_ref[:, COL_SBQ:COL_SBQ + SB_WIDTH]) * (SB_HEAD_DIM ** -0.5)).astype(BF16)
  sk_out[...] = _dot(xn, win_ref[:, COL_SBK:COL_SBK + SB_WIDTH]).astype(BF16)
  sv_out[...] = _dot(xn, win_ref[:, COL_SBV:COL_SBV + SB_WIDTH]).astype(BF16)
  return h_rw[n - 1:n, :]


def _inproj(x, cos, sin, ws, l, tm):
  b, t, _ = x.shape
  row = lambda i, j: (i, j, 0)
  full = lambda a: _layer_spec(a, l)
  weights = [ws[n] for n in ("ng", "win", "mu", "wup", "w0", "aup", "a0", "gup", "kk", "ka",
                             "cqg", "ckvg", "wuq", "wuk", "wuv", "qg", "kg", "qb", "kb")]
  rw = jax.ShapeDtypeStruct((b, t, RW_WIDTH), F32)
  att4 = jax.ShapeDtypeStruct((b, t, MLA_HEADS * MLA_SLOT), BF16)
  att2 = jax.ShapeDtypeStruct((b, t, SB_WIDTH), BF16)
  out_shape = (rw,) * 7 + (att4, att4, att2, att2, att2, att2)
  out_specs = tuple(pl.BlockSpec((1, tm, s.shape[-1]), row) for s in out_shape)
  return pl.pallas_call(
      _inproj_kernel,
      out_shape=out_shape,
      grid=(b, t // tm),
      in_specs=[pl.BlockSpec((1, tm, D_MODEL), row),
                pl.BlockSpec((1, tm, MLA_SLOT), row),
                pl.BlockSpec((1, tm, MLA_SLOT), row)] + [full(w) for w in weights],
      out_specs=out_specs,
      scratch_shapes=[pltpu.VMEM((1, RW_COLS), F32)],
      compiler_params=pltpu.CompilerParams(
          dimension_semantics=("arbitrary", "arbitrary"), vmem_limit_bytes=VMEM_LIMIT_BYTES),
      name="inproj",
  )(x, cos, sin, *weights)


RW_PASSES_SCORE = 1
RW_PASSES_INV = 1
RW_PASSES_MID = 1
RW_PASSES_STATE = 1


def _rwkv_phase1(items, p_inv, p_score, p_mid):
  c = CHUNK
  n_items = len(items)
  lane = _iota((c, LANES), 1)
  lo = lane < RW_HEAD_DIM
  stack = lambda m: jnp.concatenate([jnp.where(lo, m, 0.0), jnp.where(lo, 0.0, m)], axis=0)
  r2 = _iota((2 * c, 2 * c), 0)
  c2 = _iota((2 * c, 2 * c), 1)
  eye = jnp.where(r2 == c2, 1.0, 0.0).astype(F32)
  r4 = _iota((4 * c, 4 * c), 0)
  c4 = _iota((4 * c, 4 * c), 1)
  keep = ((r4 & (c - 1)) + jnp.where(r4 >= 2 * c, 1, 0)) > (c4 & (c - 1))
  mmm = functools.partial(_mm, pa=p_mid, pb=p_mid)

  def both(key_a, key_b, out_key, parts, rows=None):
    for s in st:
      a = s[key_a] if rows is None else rows(s[key_a])
      s[out_key] = _mm(a, s[key_b], pa=parts, pb=parts)

  st = []
  for at, rt, bt, kt, v, p_end in items:
    a_st, r_st, b_st, k_st, v_st = stack(at), stack(rt), stack(bt), stack(kt), stack(v)
    rhs = jnp.concatenate([b_st, k_st], axis=0)
    sc = _mm(jnp.concatenate([a_st, r_st], axis=0), rhs, _NT, pa=p_score, pb=p_score)
    sc = jnp.where(keep, sc, 0.0)
    st.append(dict(a_st=a_st, r_st=r_st, v_st=v_st, rhs=rhs, p_end=p_end,
                   n=sc[0:2 * c, 0:2 * c], a_ak=sc[0:2 * c, 2 * c:4 * c], r_bk=sc[2 * c:4 * c, :]))
  for s in st:
    s["nd"] = jnp.where((r2 >> 3) == (c2 >> 3), s["n"], 0.0)
  both("nd", "nd", "n2", p_inv)
  both("n2", "n2", "n4", p_inv)
  both("nd", "n2", "n3", p_inv)
  for s in st:
    s["p1"] = eye + s["nd"] + s["n2"] + s["n3"]
  both("p1", "n4", "t", p_inv)
  for s in st:
    s["inv"] = s["p1"] + s["t"]
  shift = 3
  while (1 << shift) < c:
    bs = 1 << shift
    groups = range(2 * c // (2 * bs))
    lower = lambda m, bs=bs, groups=groups: jnp.concatenate(
        [m[g * 2 * bs + bs:(g + 1) * 2 * bs] for g in groups], axis=0)
    sel = ((r2 >> (shift + 1)) == (c2 >> (shift + 1))) & ((r2 >> shift) != (c2 >> shift))
    for s in st:
      s["off"] = jnp.where(sel, s["n"], 0.0)
    both("inv", "off", "t", p_inv, rows=lower)
    both("t", "inv", "t", p_inv)
    for s in st:
      inv, t = s["inv"], s["t"]
      pieces = []
      for g in groups:
        pieces += [inv[g * 2 * bs:g * 2 * bs + bs], inv[g * 2 * bs + bs:(g + 1) * 2 * bs] + t[g * bs:(g + 1) * bs]]
      s["inv"] = jnp.concatenate(pieces, axis=0)
    shift += 1
  both("a_ak", "v_st", "av", p_mid)
  for s in st:
    s["wu"] = mmm(s["inv"], jnp.concatenate([s["a_st"], s["av"]], axis=1))
  out = []
  for s in st:
    wt, ut = s["wu"][:, 0:LANES], s["wu"][:, LANES:2 * LANES]
    r_b, r_k = s["r_bk"][:, 0:2 * c], s["r_bk"][:, 2 * c:4 * c]
    rwy = mmm(r_b, s["wu"])
    y0 = rwy[:, LANES:2 * LANES] + mmm(r_k, s["v_st"])
    m = (eye + mmm(wt, s["rhs"][0:2 * c], _TN)) * s["p_end"]
    s1 = mmm(jnp.concatenate([ut, s["v_st"]], axis=0), s["rhs"], _TN) * s["p_end"]
    out.append((s["r_st"] + rwy[:, 0:LANES], y0, m, s1))
  assert len(out) == n_items
  return out


def _rwkv_kernel(r_ref, lw_ref, k_ref, v_ref, kkn_ref, b_ref, g_ref, rk_ref, lng_ref, lnb_ref,
                 o_ref, s_ref, y_ref):
  j = pl.program_id(1)
  pairs = RW_WIDTH // LANES

  @pl.when(j == 0)
  def _():
    s_ref[...] = jnp.zeros_like(s_ref)

  tc = r_ref.shape[1]
  rr = _iota((CHUNK, CHUNK), 0)
  cc = _iota((CHUNK, CHUNK), 1)
  tri = jnp.where(rr >= cc, 1.0, 0.0).astype(BF16)
  items = []
  for ci in range(tc // CHUNK):
    rows = slice(ci * CHUNK, (ci + 1) * CHUNK)
    lw = lw_ref[0, rows, :]
    lp = _mm(tri, lw, pa=1, pb=3)
    e_in = jnp.exp(lp)
    e_ex = jnp.exp(lp - lw)
    e_neg = jnp.exp(-lp)
    at = -kkn_ref[0, rows, :] * e_ex
    rt = r_ref[0, rows, :] * e_in
    bt = b_ref[0, rows, :] * e_neg
    kt = k_ref[0, rows, :] * e_neg
    v = v_ref[0, rows, :]
    p_end = e_in[CHUNK - 1:CHUNK, :]
    for p in range(pairs):
      ln = slice(p * LANES, (p + 1) * LANES)
      items.append((at[:, ln], rt[:, ln], bt[:, ln], kt[:, ln], v[:, ln], p_end[:, ln]))
  pre = _rwkv_phase1(items, RW_PASSES_INV, RW_PASSES_SCORE, RW_PASSES_MID)

  for ci in range(tc // CHUNK):
    rows = slice(ci * CHUNK, (ci + 1) * CHUNK)
    for p in range(pairs):
      rw, y0, m, s1 = pre[ci * pairs + p]
      s = s_ref[p]
      y_st = _mm(rw, s, _NT, pa=RW_PASSES_STATE, pb=RW_PASSES_STATE) + y0
      y_ref[rows, p * LANES:(p + 1) * LANES] = y_st[0:CHUNK] + y_st[CHUNK:2 * CHUNK]
      s_ref[p] = _mm(s, m, pa=RW_PASSES_STATE, pb=RW_PASSES_STATE) + s1

  y = y_ref[...]
  bd = _block_ones(RW_HEAD_DIM)
  inv_d = 1.0 / RW_HEAD_DIM
  mean = _seg_sum(y, bd) * inv_d
  d = y - mean
  var = _seg_sum(d * d, bd) * inv_d
  yn = d * lax.rsqrt(var + RW_GN_EPS) * lng_ref[...] + lnb_ref[...]
  bonus = _seg_sum(r_ref[0] * k_ref[0] * rk_ref[...], bd) * v_ref[0]
  o_ref[0] = ((yn + bonus) * g_ref[0]).astype(BF16)


def _rwkv(r, lw, k, v, kkn, bvec, g, ws, l, tc):
  b, t, _ = r.shape
  row = lambda i, j: (i, j, 0)
  seq = pl.BlockSpec((1, tc, RW_WIDTH), row)
  rk, lng, lnb = ws["rk"], ws["lng"], ws["lnb"]
  return pl.pallas_call(
      _rwkv_kernel,
      out_shape=jax.ShapeDtypeStruct((b, t, RW_WIDTH), BF16),
      grid=(b, t // tc),
      in_specs=[seq] * 7 + [_layer_spec(a, l) for a in (rk, lng, lnb)],
      out_specs=seq,
      scratch_shapes=[pltpu.VMEM((RW_WIDTH // LANES, LANES, LANES), F32),
                      pltpu.VMEM((tc, RW_WIDTH), F32)],
      compiler_params=pltpu.CompilerParams(
          dimension_semantics=("arbitrary", "arbitrary"), vmem_limit_bytes=VMEM_LIMIT_BYTES),
      name="rwkv",
  )(r, lw, k, v, kkn, bvec, g, rk, lng, lnb)


def _mla_kernel(fixed_ref, q_ref, k_ref, v_ref, o_ref, vt_ref, *, tq, tk):
  i = pl.program_id(1)
  lo = _iota((tq, LANES), 1) < MLA_V_DIM
  vlo = _iota((tk, LANES), 1) < MLA_V_DIM
  nfull = i * (tq // tk)

  def score(jb, h):
    ks = pl.multiple_of(jb * tk, tk)
    q = q_ref[0, :, h * MLA_SLOT:(h + 1) * MLA_SLOT]
    return _dot(q, k_ref[0, pl.ds(ks, tk), h * MLA_SLOT:(h + 1) * MLA_SLOT], _NT)

  def value_tile(jb, h):
    ks = pl.multiple_of(jb * tk, tk)
    vb = v_ref[0, pl.ds(ks, tk), (h // 2) * LANES:(h // 2 + 1) * LANES]
    one = jnp.ones_like(vb)
    return jnp.where(vlo, vb, one) if h % 2 == 0 else jnp.where(vlo, one, vb)

  def visible(jb):
    qc = (i * tq + _iota((tq, tk), 0)) >> CHUNK_SHIFT
    kc = (jb * tk + _iota((tq, tk), 1)) >> CHUNK_SHIFT
    return kc <= qc

  def probs(jb):
    ks = pl.multiple_of(jb * tk, tk)
    out = []
    for h in range(MLA_HEADS):
      cols = slice(h * MLA_SLOT, (h + 1) * MLA_SLOT)
      out.append(jnp.exp(_dot(k_ref[0, pl.ds(ks, tk), cols], q_ref[0, :, cols], _NT)).astype(BF16))
    return tuple(out)

  def value_rows(jb, h):
    ks = pl.multiple_of(jb * tk, tk)
    vt = vt_ref[h // 2, :, pl.ds(ks, tk)]
    one = jnp.ones((MLA_V_DIM, tk), BF16)
    return (jnp.concatenate([vt[0:MLA_V_DIM], one], axis=0) if h % 2 == 0
            else jnp.concatenate([one, vt[MLA_V_DIM:2 * MLA_V_DIM]], axis=0))

  def fixed_block(jb, carry, last):
    accs, pr = carry
    if last:
      kc = (jb * tk + _iota((tk, tq), 0)) >> CHUNK_SHIFT
      qc = (i * tq + _iota((tk, tq), 1)) >> CHUNK_SHIFT
      pr = tuple(jnp.where(kc <= qc, p.astype(F32), 0.0).astype(BF16) for p in pr)
    accs = tuple(accs[h] + _dot(value_rows(jb, h), pr[h]) for h in range(MLA_HEADS))
    return accs, (pr if last else probs(jb + 1))

  def finish_fixed(accs):
    half = MLA_V_DIM
    for p in range(MLA_HEADS // 2):
      a, b = accs[2 * p], accs[2 * p + 1]
      out_t = jnp.concatenate([a[0:half] / a[half:2 * half], b[half:2 * half] / b[0:half]], axis=0)
      o_ref[0, :, p * LANES:(p + 1) * LANES] = out_t.T.astype(BF16)

  def online_block(jb, carry, masked):
    ms, accs = carry
    new_m, new_acc = [], []
    for h in range(MLA_HEADS):
      s, tile = score(jb, h), value_tile(jb, h)
      if masked:
        s = jnp.where(visible(jb), s, NEG_INF)
      m_new = jnp.maximum(ms[h], jnp.max(s, axis=-1, keepdims=True))
      pr = jnp.exp(s - m_new)
      new_m.append(m_new)
      new_acc.append(jnp.exp(ms[h] - m_new) * accs[h] + _dot(pr.astype(BF16), tile))
    return tuple(new_m), tuple(new_acc)

  def finish(accs):
    for p in range(MLA_HEADS // 2):
      a, b = accs[2 * p], accs[2 * p + 1]
      out = jnp.where(lo, a / pltpu.roll(a, MLA_V_DIM, 1), b / pltpu.roll(b, MLA_V_DIM, 1))
      o_ref[0, :, p * LANES:(p + 1) * LANES] = out.astype(BF16)

  fixed = fixed_ref[0, 0] == 1

  @pl.when(fixed & (i == 0))
  def _():
    for p in range(MLA_HEADS // 2):
      for c0 in range(0, v_ref.shape[1], tk):
        vt_ref[p, :, c0:c0 + tk] = v_ref[0, c0:c0 + tk, p * LANES:(p + 1) * LANES].astype(F32).T.astype(BF16)

  @pl.when(fixed)
  def _():
    zeros_t = tuple(jnp.zeros((LANES, tq), F32) for _ in range(MLA_HEADS))
    carry = lax.fori_loop(0, nfull, functools.partial(fixed_block, last=False), (zeros_t, probs(0)))
    finish_fixed(fixed_block(nfull, carry, True)[0])

  @pl.when(jnp.logical_not(fixed))
  def _():
    zeros = tuple(jnp.zeros((tq, LANES), F32) for _ in range(MLA_HEADS))
    carry = (tuple(jnp.full((tq, 1), NEG_INF, F32) for _ in range(MLA_HEADS)), zeros)
    carry = lax.fori_loop(0, nfull, functools.partial(online_block, masked=False), carry)
    finish(online_block(nfull, carry, True)[1])


def _mla(fixed_shift, q, k, v, tq, tk):
  b, t, _ = q.shape
  assert tq == tk, "one masked diagonal block per query tile"
  return pl.pallas_call(
      functools.partial(_mla_kernel, tq=tq, tk=tk),
      out_shape=jax.ShapeDtypeStruct((b, t, MLA_WIDTH), BF16),
      grid=(b, t // tq),
      in_specs=[pl.BlockSpec(memory_space=pltpu.SMEM),
                pl.BlockSpec((1, tq, MLA_HEADS * MLA_SLOT), lambda bi, i: (bi, i, 0)),
                pl.BlockSpec((1, t, MLA_HEADS * MLA_SLOT), lambda bi, i: (bi, 0, 0)),
                pl.BlockSpec((1, t, MLA_WIDTH), lambda bi, i: (bi, 0, 0))],
      out_specs=pl.BlockSpec((1, tq, MLA_WIDTH), lambda bi, i: (bi, i, 0)),
      scratch_shapes=[pltpu.VMEM((MLA_HEADS // 2, LANES, t), BF16)],
      compiler_params=pltpu.CompilerParams(
          dimension_semantics=("arbitrary", "arbitrary"), vmem_limit_bytes=VMEM_LIMIT_BYTES),
      name="mla_attention",
  )(fixed_shift, q, k, v)


SB_STATIC_BLOCKS = 3


def _sb_kernel(q_ref, k_ref, v_ref, o_ref, *, tq, tk):
  for u in range(tq // tk):
    _sb_subtile(q_ref, k_ref, v_ref, o_ref, pl.program_id(1) * (tq // tk) + u, slice(u * tk, (u + 1) * tk), tk)


def _sb_subtile(q_ref, k_ref, v_ref, o_ref, i, rows, tk):
  tq = tk
  pairs = SB_HEADS // 2
  lo = _iota((tq, LANES), 1) < SB_HEAD_DIM
  vlo = _iota((tk, LANES), 1) < SB_HEAD_DIM
  rr = _iota((tk, tk + LANES), 0)
  cc = _iota((tk, tk + LANES), 1)
  tri = jnp.where((rr >= cc) | (cc >= tk), 1.0, 0.0).astype(BF16)

  def sweep(jbs, runs, accs, guard):
    qpos = i * tq + _iota((tq, tk), 0)
    stricts, kbs, vcats = [], [], []
    for jb in jbs:
      ks = pl.multiple_of(jnp.maximum(jb, 0) * tk, tk)
      kpos = jb * tk + _iota((tq, tk), 1)
      strict = kpos < qpos
      stricts.append(strict & (kpos >= 0) if guard else strict)
      kbs.append([k_ref[0, pl.ds(ks, tk), p * LANES:(p + 1) * LANES] for p in range(pairs)])
      vc = []
      for p in range(pairs):
        vb = v_ref[0, pl.ds(ks, tk), p * LANES:(p + 1) * LANES]
        vzero = jnp.zeros_like(vb)
        vc.append(jnp.concatenate([jnp.where(vlo, vb, vzero), jnp.where(vlo, vzero, vb)], axis=0))
      vcats.append(vc)
    qs = []
    for p in range(pairs):
      qp = q_ref[0, rows, p * LANES:(p + 1) * LANES]
      qzero = jnp.zeros_like(qp)
      qs += [jnp.where(lo, qp, qzero), jnp.where(lo, qzero, qp)]
    heads = range(SB_HEADS)
    z = [[_dot(qs[h], kbs[d][h // 2], _NT) for h in heads] for d in range(len(jbs))]
    ls = [[jnp.where(stricts[d], -_softplus(z[d][h]), 0.0) for h in heads] for d in range(len(jbs))]
    cs = [[_mm(ls[d][h], tri, pa=2, pb=1) for h in heads] for d in range(len(jbs))]
    runs = list(runs)
    weights = [[] for _ in range(pairs)]
    for d in range(len(jbs)):
      for h in heads:
        logw = z[d][h] + cs[d][h][:, 0:tk] + jnp.concatenate([runs[h]] * (tk // LANES), axis=-1)
        weights[h // 2].append(jnp.where(stricts[d], jnp.exp(logw), 0.0).astype(BF16))
        runs[h] = runs[h] + cs[d][h][:, tk:tk + LANES]
    new_accs = []
    for p in range(pairs):
      vall = jnp.concatenate([vcats[d][p] for d in range(len(jbs))], axis=0)
      new_accs.append(accs[p] + _dot(jnp.concatenate(weights[p], axis=1), vall))
    return tuple(runs), tuple(new_accs)

  zr = jnp.zeros((tq, LANES), F32)
  start = (i + 1) * (tq // tk) - 1
  runs, accs = sweep([start - d for d in range(SB_STATIC_BLOCKS)], (zr,) * SB_HEADS, (zr,) * pairs, True)

  def cond(state):
    jb, runs, _ = state
    top = functools.reduce(jnp.maximum, runs)
    return (jb >= 0) & (jnp.max(top) > SB_LOG_FLOOR)

  def body(state):
    jb, runs, accs = state
    runs, accs = sweep([jb], runs, accs, False)
    return jb - 1, runs, accs

  _, _, accs = lax.while_loop(cond, body, (start - SB_STATIC_BLOCKS, runs, accs))
  for p in range(pairs):
    o_ref[0, rows, p * LANES:(p + 1) * LANES] = accs[p].astype(BF16)


def _sb(q, k, v, tq, tk):
  b, t, _ = q.shape
  return pl.pallas_call(
      functools.partial(_sb_kernel, tq=tq, tk=tk),
      out_shape=jax.ShapeDtypeStruct((b, t, SB_WIDTH), BF16),
      grid=(b, t // tq),
      in_specs=[pl.BlockSpec((1, tq, SB_WIDTH), lambda bi, i: (bi, i, 0)),
                pl.BlockSpec((1, t, SB_WIDTH), lambda bi, i: (bi, 0, 0)),
                pl.BlockSpec((1, t, SB_WIDTH), lambda bi, i: (bi, 0, 0))],
      out_specs=pl.BlockSpec((1, tq, SB_WIDTH), lambda bi, i: (bi, i, 0)),
      compiler_params=pltpu.CompilerParams(
          dimension_semantics=("arbitrary", "arbitrary"), vmem_limit_bytes=VMEM_LIMIT_BYTES),
      name="sb_attention",
  )(q, k, v)


FFN_CHUNKS = (1024, 1024, 768)
assert sum(FFN_CHUNKS) == FFN_HIDDEN


def _outffn_kernel(x_ref, yrw_ref, ymla_ref, ysb_ref, wo_ref, ng_ref, wg_ref, wu_ref, wd_ref, o_ref):
  mix = (_dot(yrw_ref[...], wo_ref[0:RW_WIDTH, :])
         + _dot(ymla_ref[...], wo_ref[RW_WIDTH:RW_WIDTH + MLA_WIDTH, :])
         + _dot(ysb_ref[...], wo_ref[RW_WIDTH + MLA_WIDTH:, :]))
  x1 = x_ref[...] + mix
  hn = (x1 * _rms_scale(x1, D_MODEL) * ng_ref[...]).astype(BF16)
  acc = x1
  start = 0
  for width in FFN_CHUNKS:
    cols = slice(start, start + width)
    gate = _dot(hn, wg_ref[:, cols])
    up = _dot(hn, wu_ref[:, cols])
    act = (gate * _sigmoid(gate) * up).astype(BF16)
    acc = acc + _dot(act, wd_ref[cols, :])
    start += width
  o_ref[...] = acc


def _outffn(x, yrw, ymla, ysb, ws, l, tm):
  n = x.shape[0]
  row = lambda i: (i, 0)
  resident = lambda a: _layer_spec(a, l, pipeline_mode=pl.Buffered(1))
  weights = [ws[k] for k in ("wo", "fg", "wg", "wu", "wd")]
  return pl.pallas_call(
      _outffn_kernel,
      out_shape=jax.ShapeDtypeStruct((n, D_MODEL), F32),
      grid=(n // tm,),
      in_specs=[pl.BlockSpec((tm, D_MODEL), row),
                pl.BlockSpec((tm, RW_WIDTH), row),
                pl.BlockSpec((tm, MLA_WIDTH), row),
                pl.BlockSpec((tm, SB_WIDTH), row)] + [resident(w) for w in weights],
      out_specs=pl.BlockSpec((tm, D_MODEL), row),
      compiler_params=pltpu.CompilerParams(
          dimension_semantics=("arbitrary",), vmem_limit_bytes=VMEM_LIMIT_BYTES),
      name="outproj_ffn",
  )(x, yrw, ymla, ysb, *weights)


def _pad_heads(w, heads, dim, slot):
  lead = w.shape[:-1]
  w = w.reshape(lead + (heads, dim))
  w = jnp.pad(w, [(0, 0)] * len(lead) + [(0, 0), (0, slot - dim)])
  return w.reshape(lead + (heads * slot,))


def _stack_weights(attn_norm_g, w_in, rw_shift_mu, rw_w_up, rw_w0, rw_a_up, rw_a0, rw_g_up, rw_k_k,
                   rw_k_a, rw_r_k, rw_ln_g, rw_ln_b, mla_cq_norm_g, mla_ckv_norm_g, mla_w_uq, mla_w_ukv,
                   mla_q_norm_g, mla_k_norm_g, w_o, ffn_norm_g, ffn_w_gate, ffn_w_up, ffn_w_down):
  depth = w_in.shape[0]
  vec = lambda a: a.reshape(depth, 1, -1)
  rest = RW_COLS
  c_q = w_in[:, :, rest:rest + MLA_Q_RANK]
  c_kv = w_in[:, :, rest + MLA_Q_RANK:rest + MLA_Q_RANK + MLA_KV_RANK]
  o = rest + MLA_Q_RANK + MLA_KV_RANK
  k_rope = w_in[:, :, o:o + MLA_ROPE_DIM]
  sb = w_in[:, :, o + MLA_ROPE_DIM:]
  rope_slot = jnp.pad(k_rope, ((0, 0), (0, 0), (MLA_NOPE_DIM, LANES - MLA_NOPE_DIM - MLA_ROPE_DIM)))
  win_p = jnp.concatenate([w_in[:, :, :RW_COLS], c_q, c_kv, sb, rope_slot], axis=2).astype(BF16)
  ukv = mla_w_ukv.reshape(depth, MLA_KV_RANK, MLA_HEADS, MLA_NOPE_DIM + MLA_V_DIM)
  wuk = _pad_heads(ukv[..., :MLA_NOPE_DIM].reshape(depth, MLA_KV_RANK, -1), MLA_HEADS, MLA_NOPE_DIM, MLA_SLOT)
  wuv = ukv[..., MLA_NOPE_DIM:].reshape(depth, MLA_KV_RANK, MLA_HEADS * MLA_V_DIM)
  head_gain = lambda g: jnp.tile(jnp.pad(g, ((0, 0), (0, MLA_SLOT - MLA_QK_DIM))), (1, MLA_HEADS)).reshape(depth, 1, -1)
  bound = (MLA_QK_DIM ** 0.5) * jnp.max(jnp.abs(mla_q_norm_g), axis=1) * jnp.max(jnp.abs(mla_k_norm_g), axis=1)
  shift = (MLA_BOUND_MARGIN * bound).reshape(depth, 1, 1)
  fixed_shift = shift <= MLA_MAX_SHIFT
  bias_lane = jnp.tile(jnp.arange(MLA_SLOT) == MLA_QK_DIM, MLA_HEADS).reshape(1, 1, -1)
  qb = jnp.where(bias_lane & fixed_shift, -shift, 0.0).astype(F32)
  kb = jnp.broadcast_to(jnp.where(bias_lane, 1.0, 0.0).astype(F32), qb.shape)
  return dict(
      qb=qb, kb=kb, fixed_shift=fixed_shift.astype(jnp.int32),
      ng=vec(attn_norm_g), win=win_p, mu=vec(rw_shift_mu), wup=rw_w_up, w0=vec(rw_w0),
      aup=rw_a_up, a0=vec(rw_a0), gup=rw_g_up, kk=vec(rw_k_k), ka=vec(rw_k_a),
      rk=vec(rw_r_k), lng=vec(rw_ln_g), lnb=vec(rw_ln_b),
      cqg=vec(mla_cq_norm_g), ckvg=vec(mla_ckv_norm_g),
      wuq=_pad_heads(mla_w_uq, MLA_HEADS, MLA_QK_DIM, MLA_SLOT).astype(BF16),
      wuk=wuk.astype(BF16), wuv=wuv.astype(BF16),
      qg=head_gain(mla_q_norm_g), kg=head_gain(mla_k_norm_g),
      wo=w_o.astype(BF16), fg=vec(ffn_norm_g),
      wg=ffn_w_gate.astype(BF16), wu=ffn_w_up.astype(BF16), wd=ffn_w_down.astype(BF16))


def _layer_spec(a, l, **kwargs):
  zeros = (0,) * (a.ndim - 1)
  return pl.BlockSpec((None,) + a.shape[1:], lambda *_: (l,) + zeros, **kwargs)


def kernel(x, positions, attn_norm_g, w_in, rw_shift_mu, rw_w_up, rw_w0, rw_a_up, rw_a0, rw_g_up, rw_k_k, rw_k_a, rw_r_k, rw_ln_g, rw_ln_b, mla_cq_norm_g, mla_ckv_norm_g, mla_w_uq, mla_w_ukv, mla_q_norm_g, mla_k_norm_g, w_o, ffn_norm_g, ffn_w_gate, ffn_w_up, ffn_w_down):
  b, t, d = x.shape
  depth = w_in.shape[0]
  params = (attn_norm_g, w_in, rw_shift_mu, rw_w_up, rw_w0, rw_a_up, rw_a0, rw_g_up, rw_k_k, rw_k_a, rw_r_k,
            rw_ln_g, rw_ln_b, mla_cq_norm_g, mla_ckv_norm_g, mla_w_uq, mla_w_ukv, mla_q_norm_g, mla_k_norm_g,
            w_o, ffn_norm_g, ffn_w_gate, ffn_w_up, ffn_w_down)
  tm = min(512, t)
  cos, sin = _rope_tables(positions, tm)
  ws = _stack_weights(*params)
  for l in range(depth):
    (r, lwd, k, v, kkn, bvec, g, mq, mk, mv, sq, sk, sv) = _inproj(x, cos, sin, ws, l, tm)
    y_rw = _rwkv(r, lwd, k, v, kkn, bvec, g, ws, l, tc=min(512, t))
    y_mla = _mla(ws["fixed_shift"][l], mq, mk, mv, tq=min(512, t), tk=min(512, t))
    y_sb = _sb(sq, sk, sv, tq=min(512, t), tk=LANES)
    x = _outffn(x.reshape(b * t, d), y_rw.reshape(b * t, -1), y_mla.reshape(b * t, -1),
                y_sb.reshape(b * t, -1), ws, l, tm=512).reshape(b, t, d)
  return x
```

```python
import functools

import jax
import jax.numpy as jnp
import numpy as np
from jax import lax
from jax.experimental import pallas as pl
from jax.experimental.pallas import tpu as pltpu

F32 = jnp.float32
BF16 = jnp.bfloat16

LANES = 128
VMEM_LIMIT_BYTES = 56 * 1024 * 1024

D_MODEL = 1024
CHUNK = 64
CHUNK_SHIFT = 6
RW_HEADS = 8
RW_HEAD_DIM = 64
RW_WIDTH = RW_HEADS * RW_HEAD_DIM
RW_DECAY_LORA = 64
RW_AAA_LORA = 64
RW_GATE_LORA = 128
RW_COLS = 3 * RW_WIDTH + RW_DECAY_LORA + RW_AAA_LORA + RW_GATE_LORA
RW_GN_EPS = 64e-5
MLA_HEADS = 4
MLA_NOPE_DIM = 64
MLA_ROPE_DIM = 32
MLA_V_DIM = 64
MLA_QK_DIM = MLA_NOPE_DIM + MLA_ROPE_DIM
MLA_Q_RANK = 256
MLA_KV_RANK = 128
MLA_WIDTH = MLA_HEADS * MLA_V_DIM
MLA_SLOT = LANES
SB_HEADS = 4
SB_HEAD_DIM = 64
SB_WIDTH = SB_HEADS * SB_HEAD_DIM
FFN_HIDDEN = 2816
ROPE_THETA = 10000.0
NORM_EPS = 1e-6
NEG_INF = -1e30
MLA_BOUND_MARGIN = 1.03
MLA_MAX_SHIFT = 41.0
SB_LOG_FLOOR = -110.0

COL_RW = 0
COL_CQ = RW_COLS
COL_CKV = COL_CQ + MLA_Q_RANK
COL_SBQ = COL_CKV + MLA_KV_RANK
COL_SBK = COL_SBQ + SB_WIDTH
COL_SBV = COL_SBK + SB_WIDTH
COL_ROPE = COL_SBV + SB_WIDTH
IN_COLS_PAD = COL_ROPE + LANES


def _split_bf16(x, parts):
  out = []
  rem = x
  for i in range(parts):
    p = rem.astype(BF16)
    out.append(p)
    if i + 1 < parts:
      rem = rem - p.astype(F32)
  return out


_NN = (((1,), (0,)), ((), ()))
_NT = (((1,), (1,)), ((), ()))
_TN = (((0,), (0,)), ((), ()))


def _dot(a, b, dims=_NN):
  return lax.dot_general(a, b, dims, preferred_element_type=F32)


def _mm(a, b, dims=_NN, pa=2, pb=2):
  as_ = _split_bf16(a, pa) if a.dtype != BF16 else [a]
  bs_ = _split_bf16(b, pb) if b.dtype != BF16 else [b]
  order = max(len(as_), len(bs_))
  acc = None
  for i, ai in enumerate(as_):
    for j, bj in enumerate(bs_):
      if i + j < order:
        t = _dot(ai, bj, dims)
        acc = t if acc is None else acc + t
  return acc


def _sigmoid(x):
  return 1.0 / (1.0 + jnp.exp(-x))


def _softplus(x):
  return jnp.maximum(x, 0.0) + jnp.log(1.0 + jnp.exp(-jnp.abs(x)))


def _iota(shape, dim):
  return lax.broadcasted_iota(jnp.int32, shape, dim)


def _block_ones(block):
  shift = block.bit_length() - 1
  r = _iota((LANES, LANES), 0) >> shift
  c = _iota((LANES, LANES), 1) >> shift
  return jnp.where(r == c, 1.0, 0.0).astype(BF16)


SEG_SUM_PARTS = 1
LORA_PARTS = 1


def _seg_sum(x, bd, fill=None):
  groups = []
  for g in range(0, x.shape[1], LANES):
    groups.append(_mm(x[:, g:g + LANES], bd, pa=SEG_SUM_PARTS, pb=1))
    if fill is not None and g % (2 * LANES) == 0:
      fill()
  return jnp.concatenate(groups, axis=1)


def _rope_kernel(pos_ref, invf_ref, cos_ref, sin_ref):
  ang = pos_ref[0].astype(F32) * invf_ref[...]
  lane = _iota(ang.shape, 1)
  first_half = lane < MLA_NOPE_DIM + MLA_ROPE_DIM // 2
  cos_ref[0] = jnp.cos(ang)
  sin_ref[0] = jnp.where(first_half, -jnp.sin(ang), jnp.sin(ang))


def _rope_tables(positions, tm):
  b, t = positions.shape
  inv_freq = ROPE_THETA ** (-jnp.arange(0, MLA_ROPE_DIM, 2, dtype=F32) / MLA_ROPE_DIM)
  invf = jnp.zeros((1, MLA_SLOT), F32)
  invf = invf.at[0, MLA_NOPE_DIM:MLA_NOPE_DIM + MLA_ROPE_DIM].set(jnp.concatenate([inv_freq, inv_freq]))
  out = jax.ShapeDtypeStruct((b, t, MLA_SLOT), F32)
  return pl.pallas_call(
      _rope_kernel,
      out_shape=(out, out),
      grid=(b, t // tm),
      in_specs=[pl.BlockSpec((1, tm, 1), lambda i, j: (i, j, 0)),
                pl.BlockSpec((1, MLA_SLOT), lambda i, j: (0, 0))],
      out_specs=(pl.BlockSpec((1, tm, MLA_SLOT), lambda i, j: (i, j, 0)),
                 pl.BlockSpec((1, tm, MLA_SLOT), lambda i, j: (i, j, 0))),
      name="rope_tables",
  )(positions.reshape(b, t, 1), invf)


INPROJ_SUBTILE = 256
INPROJ_SECTION = 256


def _rms_scale(x, width):
  ms = jnp.sum(x * x, axis=-1, keepdims=True) * (1.0 / width)
  return lax.rsqrt(ms + NORM_EPS)


def _rotary(x, cos, sin):
  half = MLA_ROPE_DIM // 2
  outs = []
  for h in range(MLA_HEADS):
    xh = x[:, h * MLA_SLOT:(h + 1) * MLA_SLOT]
    lane = _iota(xh.shape, 1)
    up = pltpu.roll(xh, half, 1)
    down = pltpu.roll(xh, MLA_SLOT - half, 1)
    partner = jnp.where(lane >= MLA_NOPE_DIM + half, up, down)
    outs.append(xh * cos + partner * sin)
  return jnp.concatenate(outs, axis=-1)


def _head_rms(x, gain, bd, fill):
  ms = _seg_sum(x * x, bd, fill) * (1.0 / MLA_QK_DIM)
  return x * lax.rsqrt(ms + NORM_EPS) * gain


def _inproj_kernel(x_ref, cos_ref, sin_ref, ng_ref, win_ref, mu_ref, wup_ref, w0_ref, aup_ref, a0_ref,
                   gup_ref, kk_ref, ka_ref, cqg_ref, ckvg_ref, wuq_ref, wuk_ref, wuv_ref, qg_ref, kg_ref,
                   qb_ref, kb_ref,
                   r_out, lw_out, k_out, v_out, kkn_out, b_out, g_out,
                   mq_out, mk_out, mv_out, sq_out, sk_out, sv_out,
                   carry_ref, ha_ref, hb_ref, *, tiles_per_row):
  j = pl.program_id(0)

  @pl.when(j == 0)
  def _():
    hb_ref[...] = jnp.zeros_like(hb_ref)
    carry_ref[...] = jnp.zeros_like(carry_ref)

  tm = x_ref.shape[0]
  sub = min(INPROJ_SUBTILE, tm)
  restart = (lax.rem(j + tiles_per_row - 1, tiles_per_row) == 0) | (j == 0)
  out_refs = (r_out, lw_out, k_out, v_out, kkn_out, b_out, g_out, mq_out, mk_out, mv_out, sq_out, sk_out, sv_out)

  def step(h_new, h_old):
    last_row = jnp.where(restart, 0.0, carry_ref[...])
    for r0 in range(0, tm, sub):
      rows = slice(r0, r0 + sub)
      x = x_ref[rows, :]
      xn = (x * _rms_scale(x, D_MODEL) * ng_ref[...]).astype(BF16)

      def sections(xn=xn, rows=rows):
        for c0 in range(0, IN_COLS_PAD, INPROJ_SECTION):
          h_new[rows, c0:c0 + INPROJ_SECTION] = _dot(xn, win_ref[:, c0:c0 + INPROJ_SECTION])
          yield
      fill = sections()
      last_row = _inproj_rows(h_old.at[rows, :], cos_ref[rows, :], sin_ref[rows, :], last_row,
                              mu_ref, wup_ref, w0_ref, aup_ref, a0_ref, gup_ref, kk_ref, ka_ref,
                              cqg_ref, ckvg_ref, wuq_ref, wuk_ref, wuv_ref, qg_ref, kg_ref, qb_ref, kb_ref,
                              tuple(o.at[rows, :] for o in out_refs), lambda fill=fill: next(fill, None))
      for _ in fill:
        pass
    carry_ref[...] = last_row

  @pl.when(lax.rem(j, 2) == 0)
  def _():
    step(ha_ref, hb_ref)

  @pl.when(lax.rem(j, 2) == 1)
  def _():
    step(hb_ref, ha_ref)


def _inproj_rows(h_ref, cos, sin, prev_row, mu_ref, wup_ref, w0_ref, aup_ref, a0_ref, gup_ref,
                 kk_ref, ka_ref, cqg_ref, ckvg_ref, wuq_ref, wuk_ref, wuv_ref, qg_ref, kg_ref, qb_ref, kb_ref, outs,
                 fill):
  (r_out, lw_out, k_out, v_out, kkn_out, b_out, g_out, mq_out, mk_out, mv_out, sq_out, sk_out, sv_out) = outs
  n = h_ref.shape[0]

  fill()
  h_rw = h_ref[:, COL_RW:COL_RW + RW_COLS]
  row = _iota(h_rw.shape, 0)
  prev = jnp.where(row == 0, prev_row, pltpu.roll(h_rw, 1, 0))
  sh = h_rw + mu_ref[...] * (prev - h_rw)
  r = sh[:, 0:RW_WIDTH]
  k = sh[:, RW_WIDTH:2 * RW_WIDTH]
  v = sh[:, 2 * RW_WIDTH:3 * RW_WIDTH]
  o = 3 * RW_WIDTH
  wd = sh[:, o:o + RW_DECAY_LORA]
  ad = sh[:, o + RW_DECAY_LORA:o + RW_DECAY_LORA + RW_AAA_LORA]
  gd = sh[:, o + RW_DECAY_LORA + RW_AAA_LORA:RW_COLS]

  lora = functools.partial(_mm, pa=LORA_PARTS, pb=LORA_PARTS)
  w_raw = -_softplus(-(w0_ref[...] + lora(jnp.tanh(wd), wup_ref[...]))) - 0.5
  lw_out[...] = -jnp.exp(w_raw)
  fill()
  a = _sigmoid(a0_ref[...] + lora(ad, aup_ref[...]))
  g_out[...] = lora(_sigmoid(gd), gup_ref[...])
  kk = k * kk_ref[...]
  bd64 = _block_ones(RW_HEAD_DIM)
  fill()
  kkn = kk * lax.rsqrt(_seg_sum(kk * kk, bd64, fill) + 1e-12)
  r_out[...] = r
  k_out[...] = k * (1.0 + (a - 1.0) * ka_ref[...])
  v_out[...] = v
  kkn_out[...] = kkn
  b_out[...] = kkn * a

  c_q = h_ref[:, COL_CQ:COL_CQ + MLA_Q_RANK]
  c_kv = h_ref[:, COL_CKV:COL_CKV + MLA_KV_RANK]
  k_rope = h_ref[:, COL_ROPE:COL_ROPE + LANES]
  cqn = (c_q * _rms_scale(c_q, MLA_Q_RANK) * cqg_ref[...]).astype(BF16)
  ckvn = (c_kv * _rms_scale(c_kv, MLA_KV_RANK) * ckvg_ref[...]).astype(BF16)
  bd128 = _block_ones(MLA_SLOT)
  fill()
  q = _head_rms(_dot(cqn, wuq_ref[...]), qg_ref[...], bd128, fill)
  fill()
  kf = _dot(ckvn, wuk_ref[...]) + jnp.concatenate([k_rope] * MLA_HEADS, axis=-1)
  kf = _head_rms(kf, kg_ref[...], bd128, fill)
  mq_out[...] = (_rotary(q, cos, sin) * (MLA_QK_DIM ** -0.5) + qb_ref[...]).astype(BF16)
  fill()
  mk_out[...] = (_rotary(kf, cos, sin) + kb_ref[...]).astype(BF16)
  mv_out[...] = _dot(ckvn, wuv_ref[...]).astype(BF16)

  sq_out[...] = (h_ref[:, COL_SBQ:COL_SBQ + SB_WIDTH] * (SB_HEAD_DIM ** -0.5)).astype(BF16)
  sk_out[...] = h_ref[:, COL_SBK:COL_SBK + SB_WIDTH].astype(BF16)
  sv_out[...] = h_ref[:, COL_SBV:COL_SBV + SB_WIDTH].astype(BF16)
  return h_rw[n - 1:n, :]


def _inproj(x, cos, sin, ws, l, tm):
  b, t, _ = x.shape
  n_tiles = b * t // tm
  cur = lambda j: (jnp.minimum(j, n_tiles - 1), 0)
  done = lambda j: (jnp.maximum(j - 1, 0), 0)
  full = lambda a: _layer_spec(a, l)
  weights = [ws[n] for n in ("ng", "win", "mu", "wup", "w0", "aup", "a0", "gup", "kk", "ka",
                             "cqg", "ckvg", "wuq", "wuk", "wuv", "qg", "kg", "qb", "kb")]
  rw = jax.ShapeDtypeStruct((b * t, RW_WIDTH), F32)
  att4 = jax.ShapeDtypeStruct((b * t, MLA_HEADS * MLA_SLOT), BF16)
  att2 = jax.ShapeDtypeStruct((b * t, SB_WIDTH), BF16)
  out_shape = (rw,) * 7 + (att4, att4, att2, att2, att2, att2)
  out_specs = tuple(pl.BlockSpec((tm, s.shape[-1]), done) for s in out_shape)
  outs = pl.pallas_call(
      functools.partial(_inproj_kernel, tiles_per_row=t // tm),
      out_shape=out_shape,
      grid=(n_tiles + 1,),
      in_specs=[pl.BlockSpec((tm, D_MODEL), cur),
                pl.BlockSpec((tm, MLA_SLOT), done),
                pl.BlockSpec((tm, MLA_SLOT), done)] + [full(w) for w in weights],
      out_specs=out_specs,
      scratch_shapes=[pltpu.VMEM((1, RW_COLS), F32),
                      pltpu.VMEM((tm, IN_COLS_PAD), F32),
                      pltpu.VMEM((tm, IN_COLS_PAD), F32)],
      compiler_params=pltpu.CompilerParams(
          dimension_semantics=("arbitrary",), vmem_limit_bytes=VMEM_LIMIT_BYTES),
      name="inproj",
  )(x.reshape(b * t, -1), cos.reshape(b * t, -1), sin.reshape(b * t, -1), *weights)
  return tuple(o.reshape(b, t, -1) for o in outs)


RW_PASSES_SCORE = 1
RW_PASSES_INV = 1
RW_PASSES_MID = 1
RW_PASSES_STATE = 1


def _rwkv_phase1(items, p_inv, p_score, p_mid):
  c = CHUNK
  n_items = len(items)
  lane = _iota((c, LANES), 1)
  lo = lane < RW_HEAD_DIM
  stack = lambda m: jnp.concatenate([jnp.where(lo, m, 0.0), jnp.where(lo, 0.0, m)], axis=0)
  r2 = _iota((2 * c, 2 * c), 0)
  c2 = _iota((2 * c, 2 * c), 1)
  eye = jnp.where(r2 == c2, 1.0, 0.0).astype(F32)
  r4 = _iota((4 * c, 4 * c), 0)
  c4 = _iota((4 * c, 4 * c), 1)
  keep = ((r4 & (c - 1)) + jnp.where(r4 >= 2 * c, 1, 0)) > (c4 & (c - 1))
  mmm = functools.partial(_mm, pa=p_mid, pb=p_mid)

  def both(key_a, key_b, out_key, parts, rows=None):
    for s in st:
      a = s[key_a] if rows is None else rows(s[key_a])
      s[out_key] = _mm(a, s[key_b], pa=parts, pb=parts)

  st = []
  for at, rt, bt, kt, v, p_end in items:
    a_st, r_st, b_st, k_st, v_st = stack(at), stack(rt), stack(bt), stack(kt), stack(v)
    rhs = jnp.concatenate([b_st, k_st], axis=0)
    sc = _mm(jnp.concatenate([a_st, r_st], axis=0), rhs, _NT, pa=p_score, pb=p_score)
    sc = jnp.where(keep, sc, 0.0)
    st.append(dict(a_st=a_st, r_st=r_st, v_st=v_st, rhs=rhs, p_end=p_end,
                   n=sc[0:2 * c, 0:2 * c], a_ak=sc[0:2 * c, 2 * c:4 * c], r_bk=sc[2 * c:4 * c, :]))
  for s in st:
    s["nd"] = jnp.where((r2 >> 3) == (c2 >> 3), s["n"], 0.0)
  both("nd", "nd", "n2", p_inv)
  both("n2", "n2", "n4", p_inv)
  both("nd", "n2", "n3", p_inv)
  for s in st:
    s["p1"] = eye + s["nd"] + s["n2"] + s["n3"]
  both("p1", "n4", "t", p_inv)
  for s in st:
    s["inv"] = s["p1"] + s["t"]
  shift = 3
  while (1 << shift) < c:
    bs = 1 << shift
    groups = range(2 * c // (2 * bs))
    lower = lambda m, bs=bs, groups=groups: jnp.concatenate(
        [m[g * 2 * bs + bs:(g + 1) * 2 * bs] for g in groups], axis=0)
    sel = ((r2 >> (shift + 1)) == (c2 >> (shift + 1))) & ((r2 >> shift) != (c2 >> shift))
    for s in st:
      s["off"] = jnp.where(sel, s["n"], 0.0)
    both("inv", "off", "t", p_inv, rows=lower)
    both("t", "inv", "t", p_inv)
    for s in st:
      inv, t = s["inv"], s["t"]
      pieces = []
      for g in groups:
        pieces += [inv[g * 2 * bs:g * 2 * bs + bs], inv[g * 2 * bs + bs:(g + 1) * 2 * bs] + t[g * bs:(g + 1) * bs]]
      s["inv"] = jnp.concatenate(pieces, axis=0)
    shift += 1
  both("a_ak", "v_st", "av", p_mid)
  for s in st:
    s["wu"] = mmm(s["inv"], jnp.concatenate([s["a_st"], s["av"]], axis=1))
  out = []
  for s in st:
    wt, ut = s["wu"][:, 0:LANES], s["wu"][:, LANES:2 * LANES]
    r_b, r_k = s["r_bk"][:, 0:2 * c], s["r_bk"][:, 2 * c:4 * c]
    rwy = mmm(r_b, s["wu"])
    y0 = rwy[:, LANES:2 * LANES] + mmm(r_k, s["v_st"])
    m = (eye + mmm(wt, s["rhs"][0:2 * c], _TN)) * s["p_end"]
    s1 = mmm(jnp.concatenate([ut, s["v_st"]], axis=0), s["rhs"], _TN) * s["p_end"]
    out.append((s["r_st"] + rwy[:, 0:LANES], y0, m, s1))
  assert len(out) == n_items
  return out


def _rwkv_kernel(r_ref, lw_ref, k_ref, v_ref, kkn_ref, b_ref, g_ref, rk_ref, lng_ref, lnb_ref,
                 o_ref, s_ref, y_ref):
  j = pl.program_id(1)
  pairs = RW_WIDTH // LANES

  @pl.when(j == 0)
  def _():
    s_ref[...] = jnp.zeros_like(s_ref)

  tc = r_ref.shape[1]
  rr = _iota((CHUNK, CHUNK), 0)
  cc = _iota((CHUNK, CHUNK), 1)
  tri = jnp.where(rr >= cc, 1.0, 0.0).astype(BF16)
  items = []
  for ci in range(tc // CHUNK):
    rows = slice(ci * CHUNK, (ci + 1) * CHUNK)
    lw = lw_ref[0, rows, :]
    lp = _mm(tri, lw, pa=1, pb=3)
    e_in = jnp.exp(lp)
    e_ex = jnp.exp(lp - lw)
    e_neg = jnp.exp(-lp)
    at = -kkn_ref[0, rows, :] * e_ex
    rt = r_ref[0, rows, :] * e_in
    bt = b_ref[0, rows, :] * e_neg
    kt = k_ref[0, rows, :] * e_neg
    v = v_ref[0, rows, :]
    p_end = e_in[CHUNK - 1:CHUNK, :]
    for p in range(pairs):
      ln = slice(p * LANES, (p + 1) * LANES)
      items.append((at[:, ln], rt[:, ln], bt[:, ln], kt[:, ln], v[:, ln], p_end[:, ln]))
  pre = _rwkv_phase1(items, RW_PASSES_INV, RW_PASSES_SCORE, RW_PASSES_MID)

  for ci in range(tc // CHUNK):
    rows = slice(ci * CHUNK, (ci + 1) * CHUNK)
    for p in range(pairs):
      rw, y0, m, s1 = pre[ci * pairs + p]
      s = s_ref[p]
      y_st = _mm(rw, s, _NT, pa=RW_PASSES_STATE, pb=RW_PASSES_STATE) + y0
      y_ref[rows, p * LANES:(p + 1) * LANES] = y_st[0:CHUNK] + y_st[CHUNK:2 * CHUNK]
      s_ref[p] = _mm(s, m, pa=RW_PASSES_STATE, pb=RW_PASSES_STATE) + s1

  y = y_ref[...]
  bd = _block_ones(RW_HEAD_DIM)
  inv_d = 1.0 / RW_HEAD_DIM
  mean = _seg_sum(y, bd) * inv_d
  d = y - mean
  var = _seg_sum(d * d, bd) * inv_d
  yn = d * lax.rsqrt(var + RW_GN_EPS) * lng_ref[...] + lnb_ref[...]
  bonus = _seg_sum(r_ref[0] * k_ref[0] * rk_ref[...], bd) * v_ref[0]
  o_ref[0] = ((yn + bonus) * g_ref[0]).astype(BF16)


def _rwkv(r, lw, k, v, kkn, bvec, g, ws, l, tc):
  b, t, _ = r.shape
  row = lambda i, j: (i, j, 0)
  seq = pl.BlockSpec((1, tc, RW_WIDTH), row)
  rk, lng, lnb = ws["rk"], ws["lng"], ws["lnb"]
  return pl.pallas_call(
      _rwkv_kernel,
      out_shape=jax.ShapeDtypeStruct((b, t, RW_WIDTH), BF16),
      grid=(b, t // tc),
      in_specs=[seq] * 7 + [_layer_spec(a, l) for a in (rk, lng, lnb)],
      out_specs=seq,
      scratch_shapes=[pltpu.VMEM((RW_WIDTH // LANES, LANES, LANES), F32),
                      pltpu.VMEM((tc, RW_WIDTH), F32)],
      compiler_params=pltpu.CompilerParams(
          dimension_semantics=("arbitrary", "arbitrary"), vmem_limit_bytes=VMEM_LIMIT_BYTES),
      name="rwkv",
  )(r, lw, k, v, kkn, bvec, g, rk, lng, lnb)


def _mla_kernel(fixed_ref, q_ref, k_ref, v_ref, o_ref, vt_ref, *, tq, tk):
  i = pl.program_id(1)
  lo = _iota((tq, LANES), 1) < MLA_V_DIM
  vlo = _iota((tk, LANES), 1) < MLA_V_DIM
  nfull = i * (tq // tk)

  def score(jb, h):
    ks = pl.multiple_of(jb * tk, tk)
    q = q_ref[0, :, h * MLA_SLOT:(h + 1) * MLA_SLOT]
    return _dot(q, k_ref[0, pl.ds(ks, tk), h * MLA_SLOT:(h + 1) * MLA_SLOT], _NT)

  def value_tile(jb, h):
    ks = pl.multiple_of(jb * tk, tk)
    vb = v_ref[0, pl.ds(ks, tk), (h // 2) * LANES:(h // 2 + 1) * LANES]
    one = jnp.ones_like(vb)
    return jnp.where(vlo, vb, one) if h % 2 == 0 else jnp.where(vlo, one, vb)

  def visible(jb):
    qc = (i * tq + _iota((tq, tk), 0)) >> CHUNK_SHIFT
    kc = (jb * tk + _iota((tq, tk), 1)) >> CHUNK_SHIFT
    return kc <= qc

  def probs(jb):
    ks = pl.multiple_of(jb * tk, tk)
    out = []
    for h in range(MLA_HEADS):
      cols = slice(h * MLA_SLOT, (h + 1) * MLA_SLOT)
      out.append(jnp.exp(_dot(k_ref[0, pl.ds(ks, tk), cols], q_ref[0, :, cols], _NT)).astype(BF16))
    return tuple(out)

  def value_rows(jb, h):
    ks = pl.multiple_of(jb * tk, tk)
    vt = vt_ref[h // 2, :, pl.ds(ks, tk)]
    one = jnp.ones((MLA_V_DIM, tk), BF16)
    return (jnp.concatenate([vt[0:MLA_V_DIM], one], axis=0) if h % 2 == 0
            else jnp.concatenate([one, vt[MLA_V_DIM:2 * MLA_V_DIM]], axis=0))

  def fixed_block(jb, carry, last):
    accs, pr = carry
    if last:
      kc = (jb * tk + _iota((tk, tq), 0)) >> CHUNK_SHIFT
      qc = (i * tq + _iota((tk, tq), 1)) >> CHUNK_SHIFT
      pr = tuple(jnp.where(kc <= qc, p.astype(F32), 0.0).astype(BF16) for p in pr)
    accs = tuple(accs[h] + _dot(value_rows(jb, h), pr[h]) for h in range(MLA_HEADS))
    return accs, (pr if last else probs(jb + 1))

  def finish_fixed(accs):
    half = MLA_V_DIM
    for p in range(MLA_HEADS // 2):
      a, b = accs[2 * p], accs[2 * p + 1]
      out_t = jnp.concatenate([a[0:half] / a[half:2 * half], b[half:2 * half] / b[0:half]], axis=0)
      o_ref[0, :, p * LANES:(p + 1) * LANES] = out_t.T.astype(BF16)

  def online_block(jb, carry, masked):
    ms, accs = carry
    new_m, new_acc = [], []
    for h in range(MLA_HEADS):
      s, tile = score(jb, h), value_tile(jb, h)
      if masked:
        s = jnp.where(visible(jb), s, NEG_INF)
      m_new = jnp.maximum(ms[h], jnp.max(s, axis=-1, keepdims=True))
      pr = jnp.exp(s - m_new)
      new_m.append(m_new)
      new_acc.append(jnp.exp(ms[h] - m_new) * accs[h] + _dot(pr.astype(BF16), tile))
    return tuple(new_m), tuple(new_acc)

  def finish(accs):
    for p in range(MLA_HEADS // 2):
      a, b = accs[2 * p], accs[2 * p + 1]
      out = jnp.where(lo, a / pltpu.roll(a, MLA_V_DIM, 1), b / pltpu.roll(b, MLA_V_DIM, 1))
      o_ref[0, :, p * LANES:(p + 1) * LANES] = out.astype(BF16)

  fixed = fixed_ref[0, 0] == 1

  @pl.when(fixed & (i == 0))
  def _():
    for p in range(MLA_HEADS // 2):
      for c0 in range(0, v_ref.shape[1], tk):
        vt_ref[p, :, c0:c0 + tk] = v_ref[0, c0:c0 + tk, p * LANES:(p + 1) * LANES].astype(F32).T.astype(BF16)

  @pl.when(fixed)
  def _():
    zeros_t = tuple(jnp.zeros((LANES, tq), F32) for _ in range(MLA_HEADS))
    carry = lax.fori_loop(0, nfull, functools.partial(fixed_block, last=False), (zeros_t, probs(0)))
    finish_fixed(fixed_block(nfull, carry, True)[0])

  @pl.when(jnp.logical_not(fixed))
  def _():
    zeros = tuple(jnp.zeros((tq, LANES), F32) for _ in range(MLA_HEADS))
    carry = (tuple(jnp.full((tq, 1), NEG_INF, F32) for _ in range(MLA_HEADS)), zeros)
    carry = lax.fori_loop(0, nfull, functools.partial(online_block, masked=False), carry)
    finish(online_block(nfull, carry, True)[1])


def _mla(fixed_shift, q, k, v, tq, tk):
  b, t, _ = q.shape
  assert tq == tk, "one masked diagonal block per query tile"
  return pl.pallas_call(
      functools.partial(_mla_kernel, tq=tq, tk=tk),
      out_shape=jax.ShapeDtypeStruct((b, t, MLA_WIDTH), BF16),
      grid=(b, t // tq),
      in_specs=[pl.BlockSpec(memory_space=pltpu.SMEM),
                pl.BlockSpec((1, tq, MLA_HEADS * MLA_SLOT), lambda bi, i: (bi, i, 0)),
                pl.BlockSpec((1, t, MLA_HEADS * MLA_SLOT), lambda bi, i: (bi, 0, 0)),
                pl.BlockSpec((1, t, MLA_WIDTH), lambda bi, i: (bi, 0, 0))],
      out_specs=pl.BlockSpec((1, tq, MLA_WIDTH), lambda bi, i: (bi, i, 0)),
      scratch_shapes=[pltpu.VMEM((MLA_HEADS // 2, LANES, t), BF16)],
      compiler_params=pltpu.CompilerParams(
          dimension_semantics=("arbitrary", "arbitrary"), vmem_limit_bytes=VMEM_LIMIT_BYTES),
      name="mla_attention",
  )(fixed_shift, q, k, v)


SB_STATIC_BLOCKS = 3
SB_CUMSUM_PARTS = 1


def _sb_kernel(q_ref, k_ref, v_ref, o_ref, *, tq, tk):
  for u in range(tq // tk):
    _sb_subtile(q_ref, k_ref, v_ref, o_ref, pl.program_id(1) * (tq // tk) + u, slice(u * tk, (u + 1) * tk), tk)


def _sb_subtile(q_ref, k_ref, v_ref, o_ref, i, rows, tk):
  tq = tk
  pairs = SB_HEADS // 2
  lo = _iota((tq, LANES), 1) < SB_HEAD_DIM
  vlo = _iota((tk, LANES), 1) < SB_HEAD_DIM
  rr = _iota((tk, tk + LANES), 0)
  cc = _iota((tk, tk + LANES), 1)
  tri = jnp.where((rr >= cc) | (cc >= tk), 1.0, 0.0).astype(BF16)

  def sweep(jbs, runs, accs, guard):
    qpos = i * tq + _iota((tq, tk), 0)
    stricts, kbs, vcats = [], [], []
    for jb in jbs:
      ks = pl.multiple_of(jnp.maximum(jb, 0) * tk, tk)
      kpos = jb * tk + _iota((tq, tk), 1)
      strict = kpos < qpos
      stricts.append(strict & (kpos >= 0) if guard else strict)
      kbs.append([k_ref[0, pl.ds(ks, tk), p * LANES:(p + 1) * LANES] for p in range(pairs)])
      vc = []
      for p in range(pairs):
        vb = v_ref[0, pl.ds(ks, tk), p * LANES:(p + 1) * LANES]
        vzero = jnp.zeros_like(vb)
        vc.append(jnp.concatenate([jnp.where(vlo, vb, vzero), jnp.where(vlo, vzero, vb)], axis=0))
      vcats.append(vc)
    qs = []
    for p in range(pairs):
      qp = q_ref[0, rows, p * LANES:(p + 1) * LANES]
      qzero = jnp.zeros_like(qp)
      qs += [jnp.where(lo, qp, qzero), jnp.where(lo, qzero, qp)]
    heads = range(SB_HEADS)
    z = [[_dot(qs[h], kbs[d][h // 2], _NT) for h in heads] for d in range(len(jbs))]
    ls = [[jnp.where(stricts[d], -_softplus(z[d][h]), 0.0) for h in heads] for d in range(len(jbs))]
    cs = [[_mm(ls[d][h], tri, pa=SB_CUMSUM_PARTS, pb=1) for h in heads] for d in range(len(jbs))]
    runs = list(runs)
    weights = [[] for _ in range(pairs)]
    for d in range(len(jbs)):
      for h in heads:
        logw = z[d][h] + cs[d][h][:, 0:tk] + jnp.concatenate([runs[h]] * (tk // LANES), axis=-1)
        weights[h // 2].append(jnp.where(stricts[d], jnp.exp(logw), 0.0).astype(BF16))
        runs[h] = runs[h] + cs[d][h][:, tk:tk + LANES]
    new_accs = []
    for p in range(pairs):
      vall = jnp.concatenate([vcats[d][p] for d in range(len(jbs))], axis=0)
      new_accs.append(accs[p] + _dot(jnp.concatenate(weights[p], axis=1), vall))
    return tuple(runs), tuple(new_accs)

  zr = jnp.zeros((tq, LANES), F32)
  start = (i + 1) * (tq // tk) - 1
  runs, accs = sweep([start - d for d in range(SB_STATIC_BLOCKS)], (zr,) * SB_HEADS, (zr,) * pairs, True)

  def cond(state):
    jb, runs, _ = state
    top = functools.reduce(jnp.maximum, runs)
    return (jb >= 0) & (jnp.max(top) > SB_LOG_FLOOR)

  def body(state):
    jb, runs, accs = state
    runs, accs = sweep([jb], runs, accs, False)
    return jb - 1, runs, accs

  _, _, accs = lax.while_loop(cond, body, (start - SB_STATIC_BLOCKS, runs, accs))
  for p in range(pairs):
    o_ref[0, rows, p * LANES:(p + 1) * LANES] = accs[p].astype(BF16)


def _sb(q, k, v, tq, tk):
  b, t, _ = q.shape
  return pl.pallas_call(
      functools.partial(_sb_kernel, tq=tq, tk=tk),
      out_shape=jax.ShapeDtypeStruct((b, t, SB_WIDTH), BF16),
      grid=(b, t // tq),
      in_specs=[pl.BlockSpec((1, tq, SB_WIDTH), lambda bi, i: (bi, i, 0)),
                pl.BlockSpec((1, t, SB_WIDTH), lambda bi, i: (bi, 0, 0)),
                pl.BlockSpec((1, t, SB_WIDTH), lambda bi, i: (bi, 0, 0))],
      out_specs=pl.BlockSpec((1, tq, SB_WIDTH), lambda bi, i: (bi, i, 0)),
      compiler_params=pltpu.CompilerParams(
          dimension_semantics=("arbitrary", "arbitrary"), vmem_limit_bytes=VMEM_LIMIT_BYTES),
      name="sb_attention",
  )(q, k, v)


FFN_CHUNKS = (1024, 1024, 768)
assert sum(FFN_CHUNKS) == FFN_HIDDEN


def _outffn_kernel(x_ref, yrw_ref, ymla_ref, ysb_ref, wo_ref, ng_ref, wg_ref, wu_ref, wd_ref, o_ref):
  mix = (_dot(yrw_ref[...], wo_ref[0:RW_WIDTH, :])
         + _dot(ymla_ref[...], wo_ref[RW_WIDTH:RW_WIDTH + MLA_WIDTH, :])
         + _dot(ysb_ref[...], wo_ref[RW_WIDTH + MLA_WIDTH:, :]))
  x1 = x_ref[...] + mix
  hn = (x1 * _rms_scale(x1, D_MODEL) * ng_ref[...]).astype(BF16)
  acc = x1
  start = 0
  for width in FFN_CHUNKS:
    cols = slice(start, start + width)
    gate = _dot(hn, wg_ref[:, cols])
    up = _dot(hn, wu_ref[:, cols])
    act = (gate * _sigmoid(gate) * up).astype(BF16)
    acc = acc + _dot(act, wd_ref[cols, :])
    start += width
  o_ref[...] = acc


def _outffn(x, yrw, ymla, ysb, ws, l, tm):
  n = x.shape[0]
  row = lambda i: (i, 0)
  resident = lambda a: _layer_spec(a, l, pipeline_mode=pl.Buffered(1))
  weights = [ws[k] for k in ("wo", "fg", "wg", "wu", "wd")]
  return pl.pallas_call(
      _outffn_kernel,
      out_shape=jax.ShapeDtypeStruct((n, D_MODEL), F32),
      grid=(n // tm,),
      in_specs=[pl.BlockSpec((tm, D_MODEL), row),
                pl.BlockSpec((tm, RW_WIDTH), row),
                pl.BlockSpec((tm, MLA_WIDTH), row),
                pl.BlockSpec((tm, SB_WIDTH), row)] + [resident(w) for w in weights],
      out_specs=pl.BlockSpec((tm, D_MODEL), row),
      compiler_params=pltpu.CompilerParams(
          dimension_semantics=("arbitrary",), vmem_limit_bytes=VMEM_LIMIT_BYTES),
      name="outproj_ffn",
  )(x, yrw, ymla, ysb, *weights)


def _pad_heads(w, heads, dim, slot):
  lead = w.shape[:-1]
  w = w.reshape(lead + (heads, dim))
  w = jnp.pad(w, [(0, 0)] * len(lead) + [(0, 0), (0, slot - dim)])
  return w.reshape(lead + (heads * slot,))


def _stack_weights(attn_norm_g, w_in, rw_shift_mu, rw_w_up, rw_w0, rw_a_up, rw_a0, rw_g_up, rw_k_k,
                   rw_k_a, rw_r_k, rw_ln_g, rw_ln_b, mla_cq_norm_g, mla_ckv_norm_g, mla_w_uq, mla_w_ukv,
                   mla_q_norm_g, mla_k_norm_g, w_o, ffn_norm_g, ffn_w_gate, ffn_w_up, ffn_w_down):
  depth = w_in.shape[0]
  vec = lambda a: a.reshape(depth, 1, -1)
  rest = RW_COLS
  c_q = w_in[:, :, rest:rest + MLA_Q_RANK]
  c_kv = w_in[:, :, rest + MLA_Q_RANK:rest + MLA_Q_RANK + MLA_KV_RANK]
  o = rest + MLA_Q_RANK + MLA_KV_RANK
  k_rope = w_in[:, :, o:o + MLA_ROPE_DIM]
  sb = w_in[:, :, o + MLA_ROPE_DIM:]
  rope_slot = jnp.pad(k_rope, ((0, 0), (0, 0), (MLA_NOPE_DIM, LANES - MLA_NOPE_DIM - MLA_ROPE_DIM)))
  win_p = jnp.concatenate([w_in[:, :, :RW_COLS], c_q, c_kv, sb, rope_slot], axis=2).astype(BF16)
  ukv = mla_w_ukv.reshape(depth, MLA_KV_RANK, MLA_HEADS, MLA_NOPE_DIM + MLA_V_DIM)
  wuk = _pad_heads(ukv[..., :MLA_NOPE_DIM].reshape(depth, MLA_KV_RANK, -1), MLA_HEADS, MLA_NOPE_DIM, MLA_SLOT)
  wuv = ukv[..., MLA_NOPE_DIM:].reshape(depth, MLA_KV_RANK, MLA_HEADS * MLA_V_DIM)
  head_gain = lambda g: jnp.tile(jnp.pad(g, ((0, 0), (0, MLA_SLOT - MLA_QK_DIM))), (1, MLA_HEADS)).reshape(depth, 1, -1)
  bound = (MLA_QK_DIM ** 0.5) * jnp.max(jnp.abs(mla_q_norm_g), axis=1) * jnp.max(jnp.abs(mla_k_norm_g), axis=1)
  shift = (MLA_BOUND_MARGIN * bound).reshape(depth, 1, 1)
  fixed_shift = shift <= MLA_MAX_SHIFT
  bias_lane = jnp.tile(jnp.arange(MLA_SLOT) == MLA_QK_DIM, MLA_HEADS).reshape(1, 1, -1)
  qb = jnp.where(bias_lane & fixed_shift, -shift, 0.0).astype(F32)
  kb = jnp.broadcast_to(jnp.where(bias_lane, 1.0, 0.0).astype(F32), qb.shape)
  return dict(
      qb=qb, kb=kb, fixed_shift=fixed_shift.astype(jnp.int32),
      ng=vec(attn_norm_g), win=win_p, mu=vec(rw_shift_mu), wup=rw_w_up, w0=vec(rw_w0),
      aup=rw_a_up, a0=vec(rw_a0), gup=rw_g_up, kk=vec(rw_k_k), ka=vec(rw_k_a),
      rk=vec(rw_r_k), lng=vec(rw_ln_g), lnb=vec(rw_ln_b),
      cqg=vec(mla_cq_norm_g), ckvg=vec(mla_ckv_norm_g),
      wuq=_pad_heads(mla_w_uq, MLA_HEADS, MLA_QK_DIM, MLA_SLOT).astype(BF16),
      wuk=wuk.astype(BF16), wuv=wuv.astype(BF16),
      qg=head_gain(mla_q_norm_g), kg=head_gain(mla_k_norm_g),
      wo=w_o.astype(BF16), fg=vec(ffn_norm_g),
      wg=ffn_w_gate.astype(BF16), wu=ffn_w_up.astype(BF16), wd=ffn_w_down.astype(BF16))


def _layer_spec(a, l, **kwargs):
  zeros = (0,) * (a.ndim - 1)
  return pl.BlockSpec((None,) + a.shape[1:], lambda *_: (l,) + zeros, **kwargs)


def kernel(x, positions, attn_norm_g, w_in, rw_shift_mu, rw_w_up, rw_w0, rw_a_up, rw_a0, rw_g_up, rw_k_k, rw_k_a, rw_r_k, rw_ln_g, rw_ln_b, mla_cq_norm_g, mla_ckv_norm_g, mla_w_uq, mla_w_ukv, mla_q_norm_g, mla_k_norm_g, w_o, ffn_norm_g, ffn_w_gate, ffn_w_up, ffn_w_down):
  b, t, d = x.shape
  depth = w_in.shape[0]
  params = (attn_norm_g, w_in, rw_shift_mu, rw_w_up, rw_w0, rw_a_up, rw_a0, rw_g_up, rw_k_k, rw_k_a, rw_r_k,
            rw_ln_g, rw_ln_b, mla_cq_norm_g, mla_ckv_norm_g, mla_w_uq, mla_w_ukv, mla_q_norm_g, mla_k_norm_g,
            w_o, ffn_norm_g, ffn_w_gate, ffn_w_up, ffn_w_down)
  tm = min(512, t)
  cos, sin = _rope_tables(positions, tm)
  ws = _stack_weights(*params)
  for l in range(depth):
    (r, lwd, k, v, kkn, bvec, g, mq, mk, mv, sq, sk, sv) = _inproj(x, cos, sin, ws, l, tm)
    y_rw = _rwkv(r, lwd, k, v, kkn, bvec, g, ws, l, tc=min(512, t))
    y_mla = _mla(ws["fixed_shift"][l], mq, mk, mv, tq=min(512, t), tk=min(512, t))
    y_sb = _sb(sq, sk, sv, tq=min(512, t), tk=LANES)
    x = _outffn(x.reshape(b * t, d), y_rw.reshape(b * t, -1), y_mla.reshape(b * t, -1),
                y_sb.reshape(b * t, -1), ws, l, tm=512).reshape(b, t, d)
  return x
```

```python
import functools

import jax
import jax.numpy as jnp
import numpy as np
from jax import lax
from jax.experimental import pallas as pl
from jax.experimental.pallas import tpu as pltpu

F32 = jnp.float32
BF16 = jnp.bfloat16

LANES = 128
VMEM_LIMIT_BYTES = 56 * 1024 * 1024

D_MODEL = 1024
CHUNK = 64
CHUNK_SHIFT = 6
RW_HEADS = 8
RW_HEAD_DIM = 64
RW_WIDTH = RW_HEADS * RW_HEAD_DIM
RW_DECAY_LORA = 64
RW_AAA_LORA = 64
RW_GATE_LORA = 128
RW_COLS = 3 * RW_WIDTH + RW_DECAY_LORA + RW_AAA_LORA + RW_GATE_LORA
RW_GN_EPS = 64e-5
MLA_HEADS = 4
MLA_NOPE_DIM = 64
MLA_ROPE_DIM = 32
MLA_V_DIM = 64
MLA_QK_DIM = MLA_NOPE_DIM + MLA_ROPE_DIM
MLA_Q_RANK = 256
MLA_KV_RANK = 128
MLA_WIDTH = MLA_HEADS * MLA_V_DIM
MLA_SLOT = LANES
SB_HEADS = 4
SB_HEAD_DIM = 64
SB_WIDTH = SB_HEADS * SB_HEAD_DIM
FFN_HIDDEN = 2816
ROPE_THETA = 10000.0
NORM_EPS = 1e-6
NEG_INF = -1e30
MLA_BOUND_MARGIN = 1.03
MLA_MAX_SHIFT = 41.0
SB_LOG_FLOOR = -110.0

COL_RW = 0
COL_CQ = RW_COLS
COL_CKV = COL_CQ + MLA_Q_RANK
COL_SBQ = COL_CKV + MLA_KV_RANK
COL_SBK = COL_SBQ + SB_WIDTH
COL_SBV = COL_SBK + SB_WIDTH
COL_ROPE = COL_SBV + SB_WIDTH
IN_COLS_PAD = COL_ROPE + LANES


def _split_bf16(x, parts):
  out = []
  rem = x
  for i in range(parts):
    p = rem.astype(BF16)
    out.append(p)
    if i + 1 < parts:
      rem = rem - p.astype(F32)
  return out


_NN = (((1,), (0,)), ((), ()))
_NT = (((1,), (1,)), ((), ()))
_TN = (((0,), (0,)), ((), ()))


def _dot(a, b, dims=_NN):
  return lax.dot_general(a, b, dims, preferred_element_type=F32)


def _mm(a, b, dims=_NN, pa=2, pb=2):
  as_ = _split_bf16(a, pa) if a.dtype != BF16 else [a]
  bs_ = _split_bf16(b, pb) if b.dtype != BF16 else [b]
  order = max(len(as_), len(bs_))
  acc = None
  for i, ai in enumerate(as_):
    for j, bj in enumerate(bs_):
      if i + j < order:
        t = _dot(ai, bj, dims)
        acc = t if acc is None else acc + t
  return acc


def _sigmoid(x):
  return 1.0 / (1.0 + jnp.exp(-x))


def _softplus(x):
  return jnp.maximum(x, 0.0) + jnp.log(1.0 + jnp.exp(-jnp.abs(x)))


def _iota(shape, dim):
  return lax.broadcasted_iota(jnp.int32, shape, dim)


def _block_ones(block):
  shift = block.bit_length() - 1
  r = _iota((LANES, LANES), 0) >> shift
  c = _iota((LANES, LANES), 1) >> shift
  return jnp.where(r == c, 1.0, 0.0).astype(BF16)


SEG_SUM_PARTS = 1
LORA_PARTS = 1


def _seg_sum(x, bd, fill=None):
  groups = []
  for g in range(0, x.shape[1], LANES):
    groups.append(_mm(x[:, g:g + LANES], bd, pa=SEG_SUM_PARTS, pb=1))
    if fill is not None and g % (2 * LANES) == 0:
      fill()
  return jnp.concatenate(groups, axis=1)


def _rope_kernel(pos_ref, invf_ref, cos_ref, sin_ref):
  ang = pos_ref[0].astype(F32) * invf_ref[...]
  lane = _iota(ang.shape, 1)
  first_half = lane < MLA_NOPE_DIM + MLA_ROPE_DIM // 2
  cos_ref[0] = jnp.cos(ang)
  sin_ref[0] = jnp.where(first_half, -jnp.sin(ang), jnp.sin(ang))


def _rope_tables(positions, tm):
  b, t = positions.shape
  inv_freq = ROPE_THETA ** (-jnp.arange(0, MLA_ROPE_DIM, 2, dtype=F32) / MLA_ROPE_DIM)
  invf = jnp.zeros((1, MLA_SLOT), F32)
  invf = invf.at[0, MLA_NOPE_DIM:MLA_NOPE_DIM + MLA_ROPE_DIM].set(jnp.concatenate([inv_freq, inv_freq]))
  out = jax.ShapeDtypeStruct((b, t, MLA_SLOT), F32)
  return pl.pallas_call(
      _rope_kernel,
      out_shape=(out, out),
      grid=(b, t // tm),
      in_specs=[pl.BlockSpec((1, tm, 1), lambda i, j: (i, j, 0)),
                pl.BlockSpec((1, MLA_SLOT), lambda i, j: (0, 0))],
      out_specs=(pl.BlockSpec((1, tm, MLA_SLOT), lambda i, j: (i, j, 0)),
                 pl.BlockSpec((1, tm, MLA_SLOT), lambda i, j: (i, j, 0))),
      name="rope_tables",
  )(positions.reshape(b, t, 1), invf)


INPROJ_SUBTILE = 256
INPROJ_SECTION = 256


def _rms_scale(x, width):
  ms = jnp.sum(x * x, axis=-1, keepdims=True) * (1.0 / width)
  return lax.rsqrt(ms + NORM_EPS)


def _rotary(x, cos, sin):
  half = MLA_ROPE_DIM // 2
  outs = []
  for h in range(MLA_HEADS):
    xh = x[:, h * MLA_SLOT:(h + 1) * MLA_SLOT]
    lane = _iota(xh.shape, 1)
    up = pltpu.roll(xh, half, 1)
    down = pltpu.roll(xh, MLA_SLOT - half, 1)
    partner = jnp.where(lane >= MLA_NOPE_DIM + half, up, down)
    outs.append(xh * cos + partner * sin)
  return jnp.concatenate(outs, axis=-1)


def _head_rms(x, gain, bd, fill):
  ms = _seg_sum(x * x, bd, fill) * (1.0 / MLA_QK_DIM)
  return x * lax.rsqrt(ms + NORM_EPS) * gain


def _inproj_kernel(x_ref, cos_ref, sin_ref, ng_ref, win_ref, mu_ref, wup_ref, w0_ref, aup_ref, a0_ref,
                   gup_ref, kk_ref, ka_ref, cqg_ref, ckvg_ref, wuq_ref, wuk_ref, wuv_ref, qg_ref, kg_ref,
                   qb_ref, kb_ref,
                   r_out, lw_out, k_out, v_out, kkn_out, b_out, g_out,
                   mq_out, mk_out, mv_out, sq_out, sk_out, sv_out,
                   carry_ref, ha_ref, hb_ref, *, tiles_per_row):
  j = pl.program_id(0)

  @pl.when(j == 0)
  def _():
    hb_ref[...] = jnp.zeros_like(hb_ref)
    carry_ref[...] = jnp.zeros_like(carry_ref)

  tm = x_ref.shape[0]
  sub = min(INPROJ_SUBTILE, tm)
  restart = (lax.rem(j + tiles_per_row - 1, tiles_per_row) == 0) | (j == 0)
  out_refs = (r_out, lw_out, k_out, v_out, kkn_out, b_out, g_out, mq_out, mk_out, mv_out, sq_out, sk_out, sv_out)

  def step(h_new, h_old):
    last_row = jnp.where(restart, 0.0, carry_ref[...])
    for r0 in range(0, tm, sub):
      rows = slice(r0, r0 + sub)
      x = x_ref[rows, :]
      xn = (x * _rms_scale(x, D_MODEL) * ng_ref[...]).astype(BF16)

      def sections(xn=xn, rows=rows):
        for c0 in range(0, IN_COLS_PAD, INPROJ_SECTION):
          h_new[rows, c0:c0 + INPROJ_SECTION] = _dot(xn, win_ref[:, c0:c0 + INPROJ_SECTION])
          yield
      fill = sections()
      last_row = _inproj_rows(h_old.at[rows, :], cos_ref[rows, :], sin_ref[rows, :], last_row,
                              mu_ref, wup_ref, w0_ref, aup_ref, a0_ref, gup_ref, kk_ref, ka_ref,
                              cqg_ref, ckvg_ref, wuq_ref, wuk_ref, wuv_ref, qg_ref, kg_ref, qb_ref, kb_ref,
                              tuple(o.at[rows, :] for o in out_refs), lambda fill=fill: next(fill, None))
      for _ in fill:
        pass
    carry_ref[...] = last_row

  @pl.when(lax.rem(j, 2) == 0)
  def _():
    step(ha_ref, hb_ref)

  @pl.when(lax.rem(j, 2) == 1)
  def _():
    step(hb_ref, ha_ref)


def _inproj_rows(h_ref, cos, sin, prev_row, mu_ref, wup_ref, w0_ref, aup_ref, a0_ref, gup_ref,
                 kk_ref, ka_ref, cqg_ref, ckvg_ref, wuq_ref, wuk_ref, wuv_ref, qg_ref, kg_ref, qb_ref, kb_ref, outs,
                 fill):
  (r_out, lw_out, k_out, v_out, kkn_out, b_out, g_out, mq_out, mk_out, mv_out, sq_out, sk_out, sv_out) = outs
  n = h_ref.shape[0]

  fill()
  h_rw = h_ref[:, COL_RW:COL_RW + RW_COLS]
  row = _iota(h_rw.shape, 0)
  prev = jnp.where(row == 0, prev_row, pltpu.roll(h_rw, 1, 0))
  sh = h_rw + mu_ref[...] * (prev - h_rw)
  r = sh[:, 0:RW_WIDTH]
  k = sh[:, RW_WIDTH:2 * RW_WIDTH]
  v = sh[:, 2 * RW_WIDTH:3 * RW_WIDTH]
  o = 3 * RW_WIDTH
  wd = sh[:, o:o + RW_DECAY_LORA]
  ad = sh[:, o + RW_DECAY_LORA:o + RW_DECAY_LORA + RW_AAA_LORA]
  gd = sh[:, o + RW_DECAY_LORA + RW_AAA_LORA:RW_COLS]

  lora = functools.partial(_mm, pa=LORA_PARTS, pb=LORA_PARTS)
  w_raw = -_softplus(-(w0_ref[...] + lora(jnp.tanh(wd), wup_ref[...]))) - 0.5
  lw_out[...] = -jnp.exp(w_raw)
  fill()
  a = _sigmoid(a0_ref[...] + lora(ad, aup_ref[...]))
  g_out[...] = lora(_sigmoid(gd), gup_ref[...])
  kk = k * kk_ref[...]
  bd64 = _block_ones(RW_HEAD_DIM)
  fill()
  kkn = kk * lax.rsqrt(_seg_sum(kk * kk, bd64, fill) + 1e-12)
  r_out[...] = r
  k_out[...] = k * (1.0 + (a - 1.0) * ka_ref[...])
  v_out[...] = v
  kkn_out[...] = kkn
  b_out[...] = kkn * a

  c_q = h_ref[:, COL_CQ:COL_CQ + MLA_Q_RANK]
  c_kv = h_ref[:, COL_CKV:COL_CKV + MLA_KV_RANK]
  k_rope = h_ref[:, COL_ROPE:COL_ROPE + LANES]
  cqn = (c_q * _rms_scale(c_q, MLA_Q_RANK) * cqg_ref[...]).astype(BF16)
  ckvn = (c_kv * _rms_scale(c_kv, MLA_KV_RANK) * ckvg_ref[...]).astype(BF16)
  bd128 = _block_ones(MLA_SLOT)
  fill()
  q = _head_rms(_dot(cqn, wuq_ref[...]), qg_ref[...], bd128, fill)
  fill()
  kf = _dot(ckvn, wuk_ref[...]) + jnp.concatenate([k_rope] * MLA_HEADS, axis=-1)
  kf = _head_rms(kf, kg_ref[...], bd128, fill)
  mq_out[...] = (_rotary(q, cos, sin) * (MLA_QK_DIM ** -0.5) + qb_ref[...]).astype(BF16)
  fill()
  mk_out[...] = (_rotary(kf, cos, sin) + kb_ref[...]).astype(BF16)
  mv_out[...] = _dot(ckvn, wuv_ref[...]).astype(BF16)

  sq_out[...] = (h_ref[:, COL_SBQ:COL_SBQ + SB_WIDTH] * (SB_HEAD_DIM ** -0.5)).astype(BF16)
  sk_out[...] = h_ref[:, COL_SBK:COL_SBK + SB_WIDTH].astype(BF16)
  sv_out[...] = h_ref[:, COL_SBV:COL_SBV + SB_WIDTH].astype(BF16)
  return h_rw[n - 1:n, :]


def _inproj(x, cos, sin, ws, l, tm):
  b, t, _ = x.shape
  n_tiles = b * t // tm
  cur = lambda j: (jnp.minimum(j, n_tiles - 1), 0)
  done = lambda j: (jnp.maximum(j - 1, 0), 0)
  full = lambda a: _layer_spec(a, l)
  weights = [ws[n] for n in ("ng", "win", "mu", "wup", "w0", "aup", "a0", "gup", "kk", "ka",
                             "cqg", "ckvg", "wuq", "wuk", "wuv", "qg", "kg", "qb", "kb")]
  rw = jax.ShapeDtypeStruct((b * t, RW_WIDTH), F32)
  att4 = jax.ShapeDtypeStruct((b * t, MLA_HEADS * MLA_SLOT), BF16)
  att2 = jax.ShapeDtypeStruct((b * t, SB_WIDTH), BF16)
  out_shape = (rw,) * 7 + (att4, att4, att2, att2, att2, att2)
  out_specs = tuple(pl.BlockSpec((tm, s.shape[-1]), done) for s in out_shape)
  outs = pl.pallas_call(
      functools.partial(_inproj_kernel, tiles_per_row=t // tm),
      out_shape=out_shape,
      grid=(n_tiles + 1,),
      in_specs=[pl.BlockSpec((tm, D_MODEL), cur),
                pl.BlockSpec((tm, MLA_SLOT), done),
                pl.BlockSpec((tm, MLA_SLOT), done)] + [full(w) for w in weights],
      out_specs=out_specs,
      scratch_shapes=[pltpu.VMEM((1, RW_COLS), F32),
                      pltpu.VMEM((tm, IN_COLS_PAD), F32),
                      pltpu.VMEM((tm, IN_COLS_PAD), F32)],
      compiler_params=pltpu.CompilerParams(
          dimension_semantics=("arbitrary",), vmem_limit_bytes=VMEM_LIMIT_BYTES),
      name="inproj",
  )(x.reshape(b * t, -1), cos.reshape(b * t, -1), sin.reshape(b * t, -1), *weights)
  return tuple(o.reshape(b, t, -1) for o in outs)


RW_PASSES_SCORE = 1
RW_PASSES_INV = 1
RW_PASSES_MID = 1
RW_PASSES_STATE = 1


def _rwkv_phase1(items, p_inv, p_score, p_mid):
  c = CHUNK
  n_items = len(items)
  lane = _iota((c, LANES), 1)
  lo = lane < RW_HEAD_DIM
  stack = lambda m: jnp.concatenate([jnp.where(lo, m, 0.0), jnp.where(lo, 0.0, m)], axis=0)
  r2 = _iota((2 * c, 2 * c), 0)
  c2 = _iota((2 * c, 2 * c), 1)
  eye = jnp.where(r2 == c2, 1.0, 0.0).astype(F32)
  r4 = _iota((4 * c, 4 * c), 0)
  c4 = _iota((4 * c, 4 * c), 1)
  keep = ((r4 & (c - 1)) + jnp.where(r4 >= 2 * c, 1, 0)) > (c4 & (c - 1))
  mmm = functools.partial(_mm, pa=p_mid, pb=p_mid)

  def both(key_a, key_b, out_key, parts, rows=None):
    for s in st:
      a = s[key_a] if rows is None else rows(s[key_a])
      s[out_key] = _mm(a, s[key_b], pa=parts, pb=parts)

  st = []
  for at, rt, bt, kt, v, p_end in items:
    a_st, r_st, b_st, k_st, v_st = stack(at), stack(rt), stack(bt), stack(kt), stack(v)
    rhs = jnp.concatenate([b_st, k_st], axis=0)
    sc = _mm(jnp.concatenate([a_st, r_st], axis=0), rhs, _NT, pa=p_score, pb=p_score)
    sc = jnp.where(keep, sc, 0.0)
    st.append(dict(a_st=a_st, r_st=r_st, v_st=v_st, rhs=rhs, p_end=p_end,
                   n=sc[0:2 * c, 0:2 * c], a_ak=sc[0:2 * c, 2 * c:4 * c], r_bk=sc[2 * c:4 * c, :]))
  for s in st:
    s["nd"] = jnp.where((r2 >> 3) == (c2 >> 3), s["n"], 0.0)
  both("nd", "nd", "n2", p_inv)
  both("n2", "n2", "n4", p_inv)
  both("nd", "n2", "n3", p_inv)
  for s in st:
    s["p1"] = eye + s["nd"] + s["n2"] + s["n3"]
  both("p1", "n4", "t", p_inv)
  for s in st:
    s["inv"] = s["p1"] + s["t"]
  shift = 3
  while (1 << shift) < c:
    bs = 1 << shift
    groups = range(2 * c // (2 * bs))
    lower = lambda m, bs=bs, groups=groups: jnp.concatenate(
        [m[g * 2 * bs + bs:(g + 1) * 2 * bs] for g in groups], axis=0)
    sel = ((r2 >> (shift + 1)) == (c2 >> (shift + 1))) & ((r2 >> shift) != (c2 >> shift))
    for s in st:
      s["off"] = jnp.where(sel, s["n"], 0.0)
    both("inv", "off", "t", p_inv, rows=lower)
    both("t", "inv", "t", p_inv)
    for s in st:
      inv, t = s["inv"], s["t"]
      pieces = []
      for g in groups:
        pieces += [inv[g * 2 * bs:g * 2 * bs + bs], inv[g * 2 * bs + bs:(g + 1) * 2 * bs] + t[g * bs:(g + 1) * bs]]
      s["inv"] = jnp.concatenate(pieces, axis=0)
    shift += 1
  both("a_ak", "v_st", "av", p_mid)
  for s in st:
    s["wu"] = mmm(s["inv"], jnp.concatenate([s["a_st"], s["av"]], axis=1))
  out = []
  for s in st:
    wt, ut = s["wu"][:, 0:LANES], s["wu"][:, LANES:2 * LANES]
    r_b, r_k = s["r_bk"][:, 0:2 * c], s["r_bk"][:, 2 * c:4 * c]
    rwy = mmm(r_b, s["wu"])
    y0 = rwy[:, LANES:2 * LANES] + mmm(r_k, s["v_st"])
    m = (eye + mmm(wt, s["rhs"][0:2 * c], _TN)) * s["p_end"]
    s1 = mmm(jnp.concatenate([ut, s["v_st"]], axis=0), s["rhs"], _TN) * s["p_end"]
    out.append((s["r_st"] + rwy[:, 0:LANES], y0, m, s1))
  assert len(out) == n_items
  return out


def _rwkv_kernel(r_ref, lw_ref, k_ref, v_ref, kkn_ref, b_ref, g_ref, rk_ref, lng_ref, lnb_ref,
                 o_ref, s_ref, y_ref):
  j = pl.program_id(1)
  pairs = RW_WIDTH // LANES

  @pl.when(j == 0)
  def _():
    s_ref[...] = jnp.zeros_like(s_ref)

  tc = r_ref.shape[1]
  rr = _iota((CHUNK, CHUNK), 0)
  cc = _iota((CHUNK, CHUNK), 1)
  tri = jnp.where(rr >= cc, 1.0, 0.0).astype(BF16)
  items = []
  for ci in range(tc // CHUNK):
    rows = slice(ci * CHUNK, (ci + 1) * CHUNK)
    lw = lw_ref[0, rows, :]
    lp = _mm(tri, lw, pa=1, pb=3)
    e_in = jnp.exp(lp)
    e_ex = jnp.exp(lp - lw)
    e_neg = jnp.exp(-lp)
    at = -kkn_ref[0, rows, :] * e_ex
    rt = r_ref[0, rows, :] * e_in
    bt = b_ref[0, rows, :] * e_neg
    kt = k_ref[0, rows, :] * e_neg
    v = v_ref[0, rows, :]
    p_end = e_in[CHUNK - 1:CHUNK, :]
    for p in range(pairs):
      ln = slice(p * LANES, (p + 1) * LANES)
      items.append((at[:, ln], rt[:, ln], bt[:, ln], kt[:, ln], v[:, ln], p_end[:, ln]))
  pre = _rwkv_phase1(items, RW_PASSES_INV, RW_PASSES_SCORE, RW_PASSES_MID)

  for ci in range(tc // CHUNK):
    rows = slice(ci * CHUNK, (ci + 1) * CHUNK)
    for p in range(pairs):
      rw, y0, m, s1 = pre[ci * pairs + p]
      s = s_ref[p]
      y_st = _mm(rw, s, _NT, pa=RW_PASSES_STATE, pb=RW_PASSES_STATE) + y0
      y_ref[rows, p * LANES:(p + 1) * LANES] = y_st[0:CHUNK] + y_st[CHUNK:2 * CHUNK]
      s_ref[p] = _mm(s, m, pa=RW_PASSES_STATE, pb=RW_PASSES_STATE) + s1

  y = y_ref[...]
  bd = _block_ones(RW_HEAD_DIM)
  inv_d = 1.0 / RW_HEAD_DIM
  mean = _seg_sum(y, bd) * inv_d
  d = y - mean
  var = _seg_sum(d * d, bd) * inv_d
  yn = d * lax.rsqrt(var + RW_GN_EPS) * lng_ref[...] + lnb_ref[...]
  bonus = _seg_sum(r_ref[0] * k_ref[0] * rk_ref[...], bd) * v_ref[0]
  o_ref[0] = ((yn + bonus) * g_ref[0]).astype(BF16)


def _rwkv(r, lw, k, v, kkn, bvec, g, ws, l, tc):
  b, t, _ = r.shape
  row = lambda i, j: (i, j, 0)
  seq = pl.BlockSpec((1, tc, RW_WIDTH), row)
  rk, lng, lnb = ws["rk"], ws["lng"], ws["lnb"]
  return pl.pallas_call(
      _rwkv_kernel,
      out_shape=jax.ShapeDtypeStruct((b, t, RW_WIDTH), BF16),
      grid=(b, t // tc),
      in_specs=[seq] * 7 + [_layer_spec(a, l) for a in (rk, lng, lnb)],
      out_specs=seq,
      scratch_shapes=[pltpu.VMEM((RW_WIDTH // LANES, LANES, LANES), F32),
                      pltpu.VMEM((tc, RW_WIDTH), F32)],
      compiler_params=pltpu.CompilerParams(
          dimension_semantics=("arbitrary", "arbitrary"), vmem_limit_bytes=VMEM_LIMIT_BYTES),
      name="rwkv",
  )(r, lw, k, v, kkn, bvec, g, rk, lng, lnb)


def _mla_kernel(fixed_ref, q_ref, k_ref, v_ref, o_ref, vt_ref, p_ref, acc_ref, *, tq, tk):
  i = pl.program_id(1)
  lo = _iota((tq, LANES), 1) < MLA_V_DIM
  vlo = _iota((tk, LANES), 1) < MLA_V_DIM
  nfull = i * (tq // tk)

  def score(jb, h):
    ks = pl.multiple_of(jb * tk, tk)
    q = q_ref[0, :, h * MLA_SLOT:(h + 1) * MLA_SLOT]
    return _dot(q, k_ref[0, pl.ds(ks, tk), h * MLA_SLOT:(h + 1) * MLA_SLOT], _NT)

  def value_tile(jb, h):
    ks = pl.multiple_of(jb * tk, tk)
    vb = v_ref[0, pl.ds(ks, tk), (h // 2) * LANES:(h // 2 + 1) * LANES]
    one = jnp.ones_like(vb)
    return jnp.where(vlo, vb, one) if h % 2 == 0 else jnp.where(vlo, one, vb)

  def visible(jb):
    qc = (i * tq + _iota((tq, tk), 0)) >> CHUNK_SHIFT
    kc = (jb * tk + _iota((tq, tk), 1)) >> CHUNK_SHIFT
    return kc <= qc

  def store_probs(jb):
    ks = pl.multiple_of(jb * tk, tk)
    for h in range(MLA_HEADS):
      cols = slice(h * MLA_SLOT, (h + 1) * MLA_SLOT)
      p_ref[h] = jnp.exp(_dot(k_ref[0, pl.ds(ks, tk), cols], q_ref[0, :, cols], _NT)).astype(BF16)

  def value_rows(jb, h):
    ks = pl.multiple_of(jb * tk, tk)
    vt = vt_ref[h // 2, :, pl.ds(ks, tk)]
    one = jnp.ones((MLA_V_DIM, tk), BF16)
    return (jnp.concatenate([vt[0:MLA_V_DIM], one], axis=0) if h % 2 == 0
            else jnp.concatenate([one, vt[MLA_V_DIM:2 * MLA_V_DIM]], axis=0))

  def fixed_block(jb, carry, last):
    for h in range(MLA_HEADS):
      pr = p_ref[h]
      if last:
        kc = (jb * tk + _iota((tk, tq), 0)) >> CHUNK_SHIFT
        qc = (i * tq + _iota((tk, tq), 1)) >> CHUNK_SHIFT
        pr = jnp.where(kc <= qc, pr.astype(F32), 0.0).astype(BF16)
      acc_ref[h] += _dot(value_rows(jb, h), pr)
    if not last:
      store_probs(jb + 1)
    return carry

  def finish_fixed():
    half = MLA_V_DIM
    for p in range(MLA_HEADS // 2):
      a, b = acc_ref[2 * p], acc_ref[2 * p + 1]
      out_t = jnp.concatenate([a[0:half] / a[half:2 * half], b[half:2 * half] / b[0:half]], axis=0)
      o_ref[0, :, p * LANES:(p + 1) * LANES] = out_t.T.astype(BF16)

  def online_block(jb, carry, masked):
    ms, accs = carry
    new_m, new_acc = [], []
    for h in range(MLA_HEADS):
      s, tile = score(jb, h), value_tile(jb, h)
      if masked:
        s = jnp.where(visible(jb), s, NEG_INF)
      m_new = jnp.maximum(ms[h], jnp.max(s, axis=-1, keepdims=True))
      pr = jnp.exp(s - m_new)
      new_m.append(m_new)
      new_acc.append(jnp.exp(ms[h] - m_new) * accs[h] + _dot(pr.astype(BF16), tile))
    return tuple(new_m), tuple(new_acc)

  def finish(accs):
    for p in range(MLA_HEADS // 2):
      a, b = accs[2 * p], accs[2 * p + 1]
      out = jnp.where(lo, a / pltpu.roll(a, MLA_V_DIM, 1), b / pltpu.roll(b, MLA_V_DIM, 1))
      o_ref[0, :, p * LANES:(p + 1) * LANES] = out.astype(BF16)

  fixed = fixed_ref[0, 0] == 1

  @pl.when(fixed & (i == 0))
  def _():
    for p in range(MLA_HEADS // 2):
      for c0 in range(0, v_ref.shape[1], tk):
        vt_ref[p, :, c0:c0 + tk] = v_ref[0, c0:c0 + tk, p * LANES:(p + 1) * LANES].astype(F32).T.astype(BF16)

  @pl.when(fixed)
  def _():
    acc_ref[...] = jnp.zeros_like(acc_ref)
    store_probs(0)
    lax.fori_loop(0, nfull, functools.partial(fixed_block, last=False), 0)
    fixed_block(nfull, 0, True)
    finish_fixed()

  @pl.when(jnp.logical_not(fixed))
  def _():
    zeros = tuple(jnp.zeros((tq, LANES), F32) for _ in range(MLA_HEADS))
    carry = (tuple(jnp.full((tq, 1), NEG_INF, F32) for _ in range(MLA_HEADS)), zeros)
    carry = lax.fori_loop(0, nfull, functools.partial(online_block, masked=False), carry)
    finish(online_block(nfull, carry, True)[1])


def _mla(fixed_shift, q, k, v, tq, tk):
  b, t, _ = q.shape
  assert tq == tk, "one masked diagonal block per query tile"
  return pl.pallas_call(
      functools.partial(_mla_kernel, tq=tq, tk=tk),
      out_shape=jax.ShapeDtypeStruct((b, t, MLA_WIDTH), BF16),
      grid=(b, t // tq),
      in_specs=[pl.BlockSpec(memory_space=pltpu.SMEM),
                pl.BlockSpec((1, tq, MLA_HEADS * MLA_SLOT), lambda bi, i: (bi, i, 0)),
                pl.BlockSpec((1, t, MLA_HEADS * MLA_SLOT), lambda bi, i: (bi, 0, 0)),
                pl.BlockSpec((1, t, MLA_WIDTH), lambda bi, i: (bi, 0, 0))],
      out_specs=pl.BlockSpec((1, tq, MLA_WIDTH), lambda bi, i: (bi, i, 0)),
      scratch_shapes=[pltpu.VMEM((MLA_HEADS // 2, LANES, t), BF16),
                      pltpu.VMEM((MLA_HEADS, tk, tq), BF16),
                      pltpu.VMEM((MLA_HEADS, LANES, tq), F32)],
      compiler_params=pltpu.CompilerParams(
          dimension_semantics=("arbitrary", "arbitrary"), vmem_limit_bytes=VMEM_LIMIT_BYTES),
      name="mla_attention",
  )(fixed_shift, q, k, v)


SB_STATIC_BLOCKS = 3
SB_CUMSUM_PARTS = 1


def _sb_kernel(q_ref, k_ref, v_ref, o_ref, *, tq, tk):
  for u in range(tq // tk):
    _sb_subtile(q_ref, k_ref, v_ref, o_ref, pl.program_id(1) * (tq // tk) + u, slice(u * tk, (u + 1) * tk), tk)


def _sb_subtile(q_ref, k_ref, v_ref, o_ref, i, rows, tk):
  tq = tk
  pairs = SB_HEADS // 2
  lo = _iota((tq, LANES), 1) < SB_HEAD_DIM
  vlo = _iota((tk, LANES), 1) < SB_HEAD_DIM
  rr = _iota((tk, tk + LANES), 0)
  cc = _iota((tk, tk + LANES), 1)
  tri = jnp.where((rr >= cc) | (cc >= tk), 1.0, 0.0).astype(BF16)

  def sweep(jbs, runs, accs, guard):
    qpos = i * tq + _iota((tq, tk), 0)
    stricts, kbs, vcats = [], [], []
    for jb in jbs:
      ks = pl.multiple_of(jnp.maximum(jb, 0) * tk, tk)
      kpos = jb * tk + _iota((tq, tk), 1)
      strict = kpos < qpos
      stricts.append(strict & (kpos >= 0) if guard else strict)
      kbs.append([k_ref[0, pl.ds(ks, tk), p * LANES:(p + 1) * LANES] for p in range(pairs)])
      vc = []
      for p in range(pairs):
        vb = v_ref[0, pl.ds(ks, tk), p * LANES:(p + 1) * LANES]
        vzero = jnp.zeros_like(vb)
        vc.append(jnp.concatenate([jnp.where(vlo, vb, vzero), jnp.where(vlo, vzero, vb)], axis=0))
      vcats.append(vc)
    qs = []
    for p in range(pairs):
      qp = q_ref[0, rows, p * LANES:(p + 1) * LANES]
      qzero = jnp.zeros_like(qp)
      qs += [jnp.where(lo, qp, qzero), jnp.where(lo, qzero, qp)]
    heads = range(SB_HEADS)
    z = [[_dot(qs[h], kbs[d][h // 2], _NT) for h in heads] for d in range(len(jbs))]
    ls = [[jnp.where(stricts[d], -_softplus(z[d][h]), 0.0) for h in heads] for d in range(len(jbs))]
    cs = [[_mm(ls[d][h], tri, pa=SB_CUMSUM_PARTS, pb=1) for h in heads] for d in range(len(jbs))]
    runs = list(runs)
    weights = [[] for _ in range(pairs)]
    for d in range(len(jbs)):
      for h in heads:
        logw = z[d][h] + cs[d][h][:, 0:tk] + jnp.concatenate([runs[h]] * (tk // LANES), axis=-1)
        weights[h // 2].append(jnp.where(stricts[d], jnp.exp(logw), 0.0).astype(BF16))
        runs[h] = runs[h] + cs[d][h][:, tk:tk + LANES]
    new_accs = []
    for p in range(pairs):
      vall = jnp.concatenate([vcats[d][p] for d in range(len(jbs))], axis=0)
      new_accs.append(accs[p] + _dot(jnp.concatenate(weights[p], axis=1), vall))
    return tuple(runs), tuple(new_accs)

  zr = jnp.zeros((tq, LANES), F32)
  start = (i + 1) * (tq // tk) - 1
  runs, accs = sweep([start - d for d in range(SB_STATIC_BLOCKS)], (zr,) * SB_HEADS, (zr,) * pairs, True)

  def cond(state):
    jb, runs, _ = state
    top = functools.reduce(jnp.maximum, runs)
    return (jb >= 0) & (jnp.max(top) > SB_LOG_FLOOR)

  def body(state):
    jb, runs, accs = state
    runs, accs = sweep([jb], runs, accs, False)
    return jb - 1, runs, accs

  _, _, accs = lax.while_loop(cond, body, (start - SB_STATIC_BLOCKS, runs, accs))
  for p in range(pairs):
    o_ref[0, rows, p * LANES:(p + 1) * LANES] = accs[p].astype(BF16)


def _sb(q, k, v, tq, tk):
  b, t, _ = q.shape
  return pl.pallas_call(
      functools.partial(_sb_kernel, tq=tq, tk=tk),
      out_shape=jax.ShapeDtypeStruct((b, t, SB_WIDTH), BF16),
      grid=(b, t // tq),
      in_specs=[pl.BlockSpec((1, tq, SB_WIDTH), lambda bi, i: (bi, i, 0)),
                pl.BlockSpec((1, t, SB_WIDTH), lambda bi, i: (bi, 0, 0)),
                pl.BlockSpec((1, t, SB_WIDTH), lambda bi, i: (bi, 0, 0))],
      out_specs=pl.BlockSpec((1, tq, SB_WIDTH), lambda bi, i: (bi, i, 0)),
      compiler_params=pltpu.CompilerParams(
          dimension_semantics=("arbitrary", "arbitrary"), vmem_limit_bytes=VMEM_LIMIT_BYTES),
      name="sb_attention",
  )(q, k, v)


FFN_CHUNKS = (1024, 1024, 768)
assert sum(FFN_CHUNKS) == FFN_HIDDEN


def _outffn_kernel(x_ref, yrw_ref, ymla_ref, ysb_ref, wo_ref, ng_ref, wg_ref, wu_ref, wd_ref, o_ref):
  mix = (_dot(yrw_ref[...], wo_ref[0:RW_WIDTH, :])
         + _dot(ymla_ref[...], wo_ref[RW_WIDTH:RW_WIDTH + MLA_WIDTH, :])
         + _dot(ysb_ref[...], wo_ref[RW_WIDTH + MLA_WIDTH:, :]))
  x1 = x_ref[...] + mix
  hn = (x1 * _rms_scale(x1, D_MODEL) * ng_ref[...]).astype(BF16)
  acc = x1
  start = 0
  for width in FFN_CHUNKS:
    cols = slice(start, start + width)
    gate = _dot(hn, wg_ref[:, cols])
    up = _dot(hn, wu_ref[:, cols])
    act = (gate * _sigmoid(gate) * up).astype(BF16)
    acc = acc + _dot(act, wd_ref[cols, :])
    start += width
  o_ref[...] = acc


def _outffn(x, yrw, ymla, ysb, ws, l, tm):
  n = x.shape[0]
  row = lambda i: (i, 0)
  resident = lambda a: _layer_spec(a, l, pipeline_mode=pl.Buffered(1))
  weights = [ws[k] for k in ("wo", "fg", "wg", "wu", "wd")]
  return pl.pallas_call(
      _outffn_kernel,
      out_shape=jax.ShapeDtypeStruct((n, D_MODEL), F32),
      grid=(n // tm,),
      in_specs=[pl.BlockSpec((tm, D_MODEL), row),
                pl.BlockSpec((tm, RW_WIDTH), row),
                pl.BlockSpec((tm, MLA_WIDTH), row),
                pl.BlockSpec((tm, SB_WIDTH), row)] + [resident(w) for w in weights],
      out_specs=pl.BlockSpec((tm, D_MODEL), row),
      compiler_params=pltpu.CompilerParams(
          dimension_semantics=("arbitrary",), vmem_limit_bytes=VMEM_LIMIT_BYTES),
      name="outproj_ffn",
  )(x, yrw, ymla, ysb, *weights)


def _pad_heads(w, heads, dim, slot):
  lead = w.shape[:-1]
  w = w.reshape(lead + (heads, dim))
  w = jnp.pad(w, [(0, 0)] * len(lead) + [(0, 0), (0, slot - dim)])
  return w.reshape(lead + (heads * slot,))


def _stack_weights(attn_norm_g, w_in, rw_shift_mu, rw_w_up, rw_w0, rw_a_up, rw_a0, rw_g_up, rw_k_k,
                   rw_k_a, rw_r_k, rw_ln_g, rw_ln_b, mla_cq_norm_g, mla_ckv_norm_g, mla_w_uq, mla_w_ukv,
                   mla_q_norm_g, mla_k_norm_g, w_o, ffn_norm_g, ffn_w_gate, ffn_w_up, ffn_w_down):
  depth = w_in.shape[0]
  vec = lambda a: a.reshape(depth, 1, -1)
  w16 = w_in.astype(BF16)
  o = RW_COLS + MLA_Q_RANK + MLA_KV_RANK
  rope_slot = jnp.pad(w16[:, :, o:o + MLA_ROPE_DIM],
                      ((0, 0), (0, 0), (MLA_NOPE_DIM, LANES - MLA_NOPE_DIM - MLA_ROPE_DIM)))
  win_p = jnp.concatenate([w16[:, :, :o], w16[:, :, o + MLA_ROPE_DIM:], rope_slot], axis=2)
  ukv = mla_w_ukv.reshape(depth, MLA_KV_RANK, MLA_HEADS, MLA_NOPE_DIM + MLA_V_DIM)
  wuk = _pad_heads(ukv[..., :MLA_NOPE_DIM].reshape(depth, MLA_KV_RANK, -1), MLA_HEADS, MLA_NOPE_DIM, MLA_SLOT)
  wuv = ukv[..., MLA_NOPE_DIM:].reshape(depth, MLA_KV_RANK, MLA_HEADS * MLA_V_DIM)
  head_gain = lambda g: jnp.tile(jnp.pad(g, ((0, 0), (0, MLA_SLOT - MLA_QK_DIM))), (1, MLA_HEADS)).reshape(depth, 1, -1)
  bound = (MLA_QK_DIM ** 0.5) * jnp.max(jnp.abs(mla_q_norm_g), axis=1) * jnp.max(jnp.abs(mla_k_norm_g), axis=1)
  shift = (MLA_BOUND_MARGIN * bound).reshape(depth, 1, 1)
  fixed_shift = shift <= MLA_MAX_SHIFT
  bias_lane = jnp.tile(jnp.arange(MLA_SLOT) == MLA_QK_DIM, MLA_HEADS).reshape(1, 1, -1)
  qb = jnp.where(bias_lane & fixed_shift, -shift, 0.0).astype(F32)
  kb = jnp.broadcast_to(jnp.where(bias_lane, 1.0, 0.0).astype(F32), qb.shape)
  return dict(
      qb=qb, kb=kb, fixed_shift=fixed_shift.astype(jnp.int32),
      ng=vec(attn_norm_g), win=win_p, mu=vec(rw_shift_mu), wup=rw_w_up, w0=vec(rw_w0),
      aup=rw_a_up, a0=vec(rw_a0), gup=rw_g_up, kk=vec(rw_k_k), ka=vec(rw_k_a),
      rk=vec(rw_r_k), lng=vec(rw_ln_g), lnb=vec(rw_ln_b),
      cqg=vec(mla_cq_norm_g), ckvg=vec(mla_ckv_norm_g),
      wuq=_pad_heads(mla_w_uq, MLA_HEADS, MLA_QK_DIM, MLA_SLOT).astype(BF16),
      wuk=wuk.astype(BF16), wuv=wuv.astype(BF16),
      qg=head_gain(mla_q_norm_g), kg=head_gain(mla_k_norm_g),
      wo=w_o.astype(BF16), fg=vec(ffn_norm_g),
      wg=ffn_w_gate.astype(BF16), wu=ffn_w_up.astype(BF16), wd=ffn_w_down.astype(BF16))


def _layer_spec(a, l, **kwargs):
  zeros = (0,) * (a.ndim - 1)
  return pl.BlockSpec((None,) + a.shape[1:], lambda *_: (l,) + zeros, **kwargs)


def kernel(x, positions, attn_norm_g, w_in, rw_shift_mu, rw_w_up, rw_w0, rw_a_up, rw_a0, rw_g_up, rw_k_k, rw_k_a, rw_r_k, rw_ln_g, rw_ln_b, mla_cq_norm_g, mla_ckv_norm_g, mla_w_uq, mla_w_ukv, mla_q_norm_g, mla_k_norm_g, w_o, ffn_norm_g, ffn_w_gate, ffn_w_up, ffn_w_down):
  b, t, d = x.shape
  depth = w_in.shape[0]
  params = (attn_norm_g, w_in, rw_shift_mu, rw_w_up, rw_w0, rw_a_up, rw_a0, rw_g_up, rw_k_k, rw_k_a, rw_r_k,
            rw_ln_g, rw_ln_b, mla_cq_norm_g, mla_ckv_norm_g, mla_w_uq, mla_w_ukv, mla_q_norm_g, mla_k_norm_g,
            w_o, ffn_norm_g, ffn_w_gate, ffn_w_up, ffn_w_down)
  tm = min(512, t)
  cos, sin = _rope_tables(positions, tm)
  ws = _stack_weights(*params)
  for l in range(depth):
    (r, lwd, k, v, kkn, bvec, g, mq, mk, mv, sq, sk, sv) = _inproj(x, cos, sin, ws, l, tm)
    y_rw = _rwkv(r, lwd, k, v, kkn, bvec, g, ws, l, tc=min(512, t))
    y_mla = _mla(ws["fixed_shift"][l], mq, mk, mv, tq=min(512, t), tk=min(512, t))
    y_sb = _sb(sq, sk, sv, tq=min(512, t), tk=LANES)
    x = _outffn(x.reshape(b * t, d), y_rw.reshape(b * t, -1), y_mla.reshape(b * t, -1),
                y_sb.reshape(b * t, -1), ws, l, tm=512).reshape(b, t, d)
  return x
```

```python
import functools

import jax
import jax.numpy as jnp
import numpy as np
from jax import lax
from jax.experimental import pallas as pl
from jax.experimental.pallas import tpu as pltpu

F32 = jnp.float32
BF16 = jnp.bfloat16

LANES = 128
VMEM_LIMIT_BYTES = 56 * 1024 * 1024

D_MODEL = 1024
CHUNK = 64
CHUNK_SHIFT = 6
RW_HEADS = 8
RW_HEAD_DIM = 64
RW_WIDTH = RW_HEADS * RW_HEAD_DIM
RW_DECAY_LORA = 64
RW_AAA_LORA = 64
RW_GATE_LORA = 128
RW_COLS = 3 * RW_WIDTH + RW_DECAY_LORA + RW_AAA_LORA + RW_GATE_LORA
RW_GN_EPS = 64e-5
MLA_HEADS = 4
MLA_NOPE_DIM = 64
MLA_ROPE_DIM = 32
MLA_V_DIM = 64
MLA_QK_DIM = MLA_NOPE_DIM + MLA_ROPE_DIM
MLA_Q_RANK = 256
MLA_KV_RANK = 128
MLA_WIDTH = MLA_HEADS * MLA_V_DIM
MLA_SLOT = LANES
SB_HEADS = 4
SB_HEAD_DIM = 64
SB_WIDTH = SB_HEADS * SB_HEAD_DIM
FFN_HIDDEN = 2816
ROPE_THETA = 10000.0
NORM_EPS = 1e-6
NEG_INF = -1e30
MLA_BOUND_MARGIN = 1.03
MLA_MAX_SHIFT = 41.0
SB_LOG_FLOOR = -110.0

COL_RW = 0
COL_CQ = RW_COLS
COL_CKV = COL_CQ + MLA_Q_RANK
COL_SBQ = COL_CKV + MLA_KV_RANK
COL_SBK = COL_SBQ + SB_WIDTH
COL_SBV = COL_SBK + SB_WIDTH
COL_ROPE = COL_SBV + SB_WIDTH
IN_COLS_PAD = COL_ROPE + LANES


def _split_bf16(x, parts):
  out = []
  rem = x
  for i in range(parts):
    p = rem.astype(BF16)
    out.append(p)
    if i + 1 < parts:
      rem = rem - p.astype(F32)
  return out


_NN = (((1,), (0,)), ((), ()))
_NT = (((1,), (1,)), ((), ()))
_TN = (((0,), (0,)), ((), ()))


def _dot(a, b, dims=_NN):
  return lax.dot_general(a, b, dims, preferred_element_type=F32)


def _mm(a, b, dims=_NN, pa=2, pb=2):
  as_ = _split_bf16(a, pa) if a.dtype != BF16 else [a]
  bs_ = _split_bf16(b, pb) if b.dtype != BF16 else [b]
  order = max(len(as_), len(bs_))
  acc = None
  for i, ai in enumerate(as_):
    for j, bj in enumerate(bs_):
      if i + j < order:
        t = _dot(ai, bj, dims)
        acc = t if acc is None else acc + t
  return acc


def _sigmoid(x):
  return 1.0 / (1.0 + jnp.exp(-x))


def _softplus(x):
  return jnp.maximum(x, 0.0) + jnp.log(1.0 + jnp.exp(-jnp.abs(x)))


def _iota(shape, dim):
  return lax.broadcasted_iota(jnp.int32, shape, dim)


def _block_ones(block):
  shift = block.bit_length() - 1
  r = _iota((LANES, LANES), 0) >> shift
  c = _iota((LANES, LANES), 1) >> shift
  return jnp.where(r == c, 1.0, 0.0).astype(BF16)


SEG_SUM_PARTS = 1
LORA_PARTS = 1


def _seg_sum(x, bd, fill=None):
  groups = []
  for g in range(0, x.shape[1], LANES):
    groups.append(_mm(x[:, g:g + LANES], bd, pa=SEG_SUM_PARTS, pb=1))
    if fill is not None and g % (2 * LANES) == 0:
      fill()
  return jnp.concatenate(groups, axis=1)


def _rope_kernel(pos_ref, invf_ref, cos_ref, sin_ref):
  ang = pos_ref[0].astype(F32) * invf_ref[...]
  lane = _iota(ang.shape, 1)
  first_half = lane < MLA_NOPE_DIM + MLA_ROPE_DIM // 2
  cos_ref[0] = jnp.cos(ang)
  sin_ref[0] = jnp.where(first_half, -jnp.sin(ang), jnp.sin(ang))


def _rope_tables(positions, tm):
  b, t = positions.shape
  inv_freq = ROPE_THETA ** (-jnp.arange(0, MLA_ROPE_DIM, 2, dtype=F32) / MLA_ROPE_DIM)
  invf = jnp.zeros((1, MLA_SLOT), F32)
  invf = invf.at[0, MLA_NOPE_DIM:MLA_NOPE_DIM + MLA_ROPE_DIM].set(jnp.concatenate([inv_freq, inv_freq]))
  out = jax.ShapeDtypeStruct((b, t, MLA_SLOT), F32)
  return pl.pallas_call(
      _rope_kernel,
      out_shape=(out, out),
      grid=(b, t // tm),
      in_specs=[pl.BlockSpec((1, tm, 1), lambda i, j: (i, j, 0)),
                pl.BlockSpec((1, MLA_SLOT), lambda i, j: (0, 0))],
      out_specs=(pl.BlockSpec((1, tm, MLA_SLOT), lambda i, j: (i, j, 0)),
                 pl.BlockSpec((1, tm, MLA_SLOT), lambda i, j: (i, j, 0))),
      name="rope_tables",
  )(positions.reshape(b, t, 1), invf)


RW_IO_DTYPE = BF16
INPROJ_SUBTILE = 256
INPROJ_SECTION = 256


def _rms_scale(x, width):
  ms = jnp.sum(x * x, axis=-1, keepdims=True) * (1.0 / width)
  return lax.rsqrt(ms + NORM_EPS)


def _rotary(x, cos, sin):
  half = MLA_ROPE_DIM // 2
  outs = []
  for h in range(MLA_HEADS):
    xh = x[:, h * MLA_SLOT:(h + 1) * MLA_SLOT]
    lane = _iota(xh.shape, 1)
    up = pltpu.roll(xh, half, 1)
    down = pltpu.roll(xh, MLA_SLOT - half, 1)
    partner = jnp.where(lane >= MLA_NOPE_DIM + half, up, down)
    outs.append(xh * cos + partner * sin)
  return jnp.concatenate(outs, axis=-1)


def _head_rms(x, gain, bd, fill):
  ms = _seg_sum(x * x, bd, fill) * (1.0 / MLA_QK_DIM)
  return x * lax.rsqrt(ms + NORM_EPS) * gain


def _inproj_kernel(x_ref, cos_ref, sin_ref, ng_ref, win_ref, wtail_ref, mu_ref, wup_ref, w0_ref, aup_ref, a0_ref,
                   gup_ref, kk_ref, ka_ref, cqg_ref, ckvg_ref, wuq_ref, wuk_ref, wuv_ref, qg_ref, kg_ref,
                   qb_ref, kb_ref,
                   r_out, lw_out, k_out, v_out, kkn_out, b_out, g_out,
                   mq_out, mk_out, mv_out, sq_out, sk_out, sv_out,
                   carry_ref, ha_ref, hb_ref, *, tiles_per_row):
  j = pl.program_id(0)

  @pl.when(j == 0)
  def _():
    hb_ref[...] = jnp.zeros_like(hb_ref)
    carry_ref[...] = jnp.zeros_like(carry_ref)

  tm = x_ref.shape[0]
  sub = min(INPROJ_SUBTILE, tm)
  restart = (lax.rem(j + tiles_per_row - 1, tiles_per_row) == 0) | (j == 0)
  out_refs = (r_out, lw_out, k_out, v_out, kkn_out, b_out, g_out, mq_out, mk_out, mv_out, sq_out, sk_out, sv_out)

  def step(h_new, h_old):
    last_row = jnp.where(restart, 0.0, carry_ref[...])
    for r0 in range(0, tm, sub):
      rows = slice(r0, r0 + sub)
      x = x_ref[rows, :]
      xn = (x * _rms_scale(x, D_MODEL) * ng_ref[...]).astype(BF16)

      def sections(xn=xn, rows=rows):
        base = 0
        for w_ref in (win_ref, wtail_ref):
          for c0 in range(0, w_ref.shape[1], INPROJ_SECTION):
            width = min(INPROJ_SECTION, w_ref.shape[1] - c0)
            h_new[rows, base + c0:base + c0 + width] = _dot(xn, w_ref[:, c0:c0 + width])
            yield
          base += w_ref.shape[1]
      fill = sections()
      last_row = _inproj_rows(h_old.at[rows, :], cos_ref[rows, :], sin_ref[rows, :], last_row,
                              mu_ref, wup_ref, w0_ref, aup_ref, a0_ref, gup_ref, kk_ref, ka_ref,
                              cqg_ref, ckvg_ref, wuq_ref, wuk_ref, wuv_ref, qg_ref, kg_ref, qb_ref, kb_ref,
                              tuple(o.at[rows, :] for o in out_refs), lambda fill=fill: next(fill, None))
      for _ in fill:
        pass
    carry_ref[...] = last_row

  @pl.when(lax.rem(j, 2) == 0)
  def _():
    step(ha_ref, hb_ref)

  @pl.when(lax.rem(j, 2) == 1)
  def _():
    step(hb_ref, ha_ref)


def _inproj_rows(h_ref, cos, sin, prev_row, mu_ref, wup_ref, w0_ref, aup_ref, a0_ref, gup_ref,
                 kk_ref, ka_ref, cqg_ref, ckvg_ref, wuq_ref, wuk_ref, wuv_ref, qg_ref, kg_ref, qb_ref, kb_ref, outs,
                 fill):
  (r_out, lw_out, k_out, v_out, kkn_out, b_out, g_out, mq_out, mk_out, mv_out, sq_out, sk_out, sv_out) = outs
  n = h_ref.shape[0]

  fill()
  h_rw = h_ref[:, COL_RW:COL_RW + RW_COLS]
  row = _iota(h_rw.shape, 0)
  prev = jnp.where(row == 0, prev_row, pltpu.roll(h_rw, 1, 0))
  sh = h_rw + mu_ref[...] * (prev - h_rw)
  r = sh[:, 0:RW_WIDTH]
  k = sh[:, RW_WIDTH:2 * RW_WIDTH]
  v = sh[:, 2 * RW_WIDTH:3 * RW_WIDTH]
  o = 3 * RW_WIDTH
  wd = sh[:, o:o + RW_DECAY_LORA]
  ad = sh[:, o + RW_DECAY_LORA:o + RW_DECAY_LORA + RW_AAA_LORA]
  gd = sh[:, o + RW_DECAY_LORA + RW_AAA_LORA:RW_COLS]

  lora = functools.partial(_mm, pa=LORA_PARTS, pb=LORA_PARTS)
  w_raw = -_softplus(-(w0_ref[...] + lora(jnp.tanh(wd), wup_ref[...]))) - 0.5
  lw_out[...] = -jnp.exp(w_raw)
  fill()
  a = _sigmoid(a0_ref[...] + lora(ad, aup_ref[...]))
  g_out[...] = lora(_sigmoid(gd), gup_ref[...]).astype(g_out.dtype)
  kk = k * kk_ref[...]
  bd64 = _block_ones(RW_HEAD_DIM)
  fill()
  kkn = kk * lax.rsqrt(_seg_sum(kk * kk, bd64, fill) + 1e-12)
  r_out[...] = r.astype(r_out.dtype)
  k_out[...] = (k * (1.0 + (a - 1.0) * ka_ref[...])).astype(k_out.dtype)
  v_out[...] = v.astype(v_out.dtype)
  kkn_out[...] = kkn.astype(kkn_out.dtype)
  b_out[...] = (kkn * a).astype(b_out.dtype)

  c_q = h_ref[:, COL_CQ:COL_CQ + MLA_Q_RANK]
  c_kv = h_ref[:, COL_CKV:COL_CKV + MLA_KV_RANK]
  k_rope = h_ref[:, COL_ROPE:COL_ROPE + LANES]
  cqn = (c_q * _rms_scale(c_q, MLA_Q_RANK) * cqg_ref[...]).astype(BF16)
  ckvn = (c_kv * _rms_scale(c_kv, MLA_KV_RANK) * ckvg_ref[...]).astype(BF16)
  bd128 = _block_ones(MLA_SLOT)
  fill()
  q = _head_rms(_dot(cqn, wuq_ref[...]), qg_ref[...], bd128, fill)
  fill()
  kf = _dot(ckvn, wuk_ref[...]) + jnp.concatenate([k_rope] * MLA_HEADS, axis=-1)
  kf = _head_rms(kf, kg_ref[...], bd128, fill)
  mq_out[...] = (_rotary(q, cos, sin) * (MLA_QK_DIM ** -0.5) + qb_ref[...]).astype(BF16)
  fill()
  mk_out[...] = (_rotary(kf, cos, sin) + kb_ref[...]).astype(BF16)
  mv_out[...] = _dot(ckvn, wuv_ref[...]).astype(BF16)

  sq_out[...] = (h_ref[:, COL_SBQ:COL_SBQ + SB_WIDTH] * (SB_HEAD_DIM ** -0.5)).astype(BF16)
  sk_out[...] = h_ref[:, COL_SBK:COL_SBK + SB_WIDTH].astype(BF16)
  sv_out[...] = h_ref[:, COL_SBV:COL_SBV + SB_WIDTH].astype(BF16)
  return h_rw[n - 1:n, :]


def _inproj(x, cos, sin, ws, l, tm):
  b, t, _ = x.shape
  n_tiles = b * t // tm
  cur = lambda j: (jnp.minimum(j, n_tiles - 1), 0)
  done = lambda j: (jnp.maximum(j - 1, 0), 0)
  full = lambda a: _layer_spec(a, l)
  weights = [ws[n] for n in ("ng", "win", "wtail", "mu", "wup", "w0", "aup", "a0", "gup", "kk", "ka",
                             "cqg", "ckvg", "wuq", "wuk", "wuv", "qg", "kg", "qb", "kb")]
  rw = jax.ShapeDtypeStruct((b * t, RW_WIDTH), RW_IO_DTYPE)
  lwd = jax.ShapeDtypeStruct((b * t, RW_WIDTH), F32)
  att4 = jax.ShapeDtypeStruct((b * t, MLA_HEADS * MLA_SLOT), BF16)
  att2 = jax.ShapeDtypeStruct((b * t, SB_WIDTH), BF16)
  out_shape = (rw, lwd) + (rw,) * 5 + (att4, att4, att2, att2, att2, att2)
  out_specs = tuple(pl.BlockSpec((tm, s.shape[-1]), done) for s in out_shape)
  outs = pl.pallas_call(
      functools.partial(_inproj_kernel, tiles_per_row=t // tm),
      out_shape=out_shape,
      grid=(n_tiles + 1,),
      in_specs=[pl.BlockSpec((tm, D_MODEL), cur),
                pl.BlockSpec((tm, MLA_SLOT), done),
                pl.BlockSpec((tm, MLA_SLOT), done)] + [full(w) for w in weights],
      out_specs=out_specs,
      scratch_shapes=[pltpu.VMEM((1, RW_COLS), F32),
                      pltpu.VMEM((tm, IN_COLS_PAD), F32),
                      pltpu.VMEM((tm, IN_COLS_PAD), F32)],
      compiler_params=pltpu.CompilerParams(
          dimension_semantics=("arbitrary",), vmem_limit_bytes=VMEM_LIMIT_BYTES),
      name="inproj",
  )(x.reshape(b * t, -1), cos.reshape(b * t, -1), sin.reshape(b * t, -1), *weights)
  return tuple(o.reshape(b, t, -1) for o in outs)


RW_PASSES_SCORE = 1
RW_PASSES_INV = 1
RW_PASSES_MID = 1
RW_PASSES_STATE = 1


def _rwkv_phase1(items, p_inv, p_score, p_mid):
  c = CHUNK
  n_items = len(items)
  lane = _iota((c, LANES), 1)
  lo = lane < RW_HEAD_DIM
  stack = lambda m: jnp.concatenate([jnp.where(lo, m, 0.0), jnp.where(lo, 0.0, m)], axis=0)
  r2 = _iota((2 * c, 2 * c), 0)
  c2 = _iota((2 * c, 2 * c), 1)
  eye = jnp.where(r2 == c2, 1.0, 0.0).astype(F32)
  r4 = _iota((4 * c, 4 * c), 0)
  c4 = _iota((4 * c, 4 * c), 1)
  keep = ((r4 & (c - 1)) + jnp.where(r4 >= 2 * c, 1, 0)) > (c4 & (c - 1))
  mmm = functools.partial(_mm, pa=p_mid, pb=p_mid)

  def both(key_a, key_b, out_key, parts, rows=None):
    for s in st:
      a = s[key_a] if rows is None else rows(s[key_a])
      s[out_key] = _mm(a, s[key_b], pa=parts, pb=parts)

  st = []
  for at, rt, bt, kt, v, p_end in items:
    a_st, r_st, b_st, k_st, v_st = stack(at), stack(rt), stack(bt), stack(kt), stack(v)
    rhs = jnp.concatenate([b_st, k_st], axis=0)
    sc = _mm(jnp.concatenate([a_st, r_st], axis=0), rhs, _NT, pa=p_score, pb=p_score)
    sc = jnp.where(keep, sc, 0.0)
    st.append(dict(a_st=a_st, r_st=r_st, v_st=v_st, rhs=rhs, p_end=p_end,
                   n=sc[0:2 * c, 0:2 * c], a_ak=sc[0:2 * c, 2 * c:4 * c], r_bk=sc[2 * c:4 * c, :]))
  for s in st:
    s["nd"] = jnp.where((r2 >> 3) == (c2 >> 3), s["n"], 0.0)
  both("nd", "nd", "n2", p_inv)
  both("n2", "n2", "n4", p_inv)
  both("nd", "n2", "n3", p_inv)
  for s in st:
    s["p1"] = eye + s["nd"] + s["n2"] + s["n3"]
  both("p1", "n4", "t", p_inv)
  for s in st:
    s["inv"] = s["p1"] + s["t"]
  shift = 3
  while (1 << shift) < c:
    bs = 1 << shift
    groups = range(2 * c // (2 * bs))
    lower = lambda m, bs=bs, groups=groups: jnp.concatenate(
        [m[g * 2 * bs + bs:(g + 1) * 2 * bs] for g in groups], axis=0)
    sel = ((r2 >> (shift + 1)) == (c2 >> (shift + 1))) & ((r2 >> shift) != (c2 >> shift))
    for s in st:
      s["off"] = jnp.where(sel, s["n"], 0.0)
    both("inv", "off", "t", p_inv, rows=lower)
    both("t", "inv", "t", p_inv)
    for s in st:
      inv, t = s["inv"], s["t"]
      pieces = []
      for g in groups:
        pieces += [inv[g * 2 * bs:g * 2 * bs + bs], inv[g * 2 * bs + bs:(g + 1) * 2 * bs] + t[g * bs:(g + 1) * bs]]
      s["inv"] = jnp.concatenate(pieces, axis=0)
    shift += 1
  both("a_ak", "v_st", "av", p_mid)
  for s in st:
    s["wu"] = mmm(s["inv"], jnp.concatenate([s["a_st"], s["av"]], axis=1))
  out = []
  for s in st:
    wt, ut = s["wu"][:, 0:LANES], s["wu"][:, LANES:2 * LANES]
    r_b, r_k = s["r_bk"][:, 0:2 * c], s["r_bk"][:, 2 * c:4 * c]
    rwy = mmm(r_b, s["wu"])
    y0 = rwy[:, LANES:2 * LANES] + mmm(r_k, s["v_st"])
    m = (eye + mmm(wt, s["rhs"][0:2 * c], _TN)) * s["p_end"]
    s1 = mmm(jnp.concatenate([ut, s["v_st"]], axis=0), s["rhs"], _TN) * s["p_end"]
    out.append((s["r_st"] + rwy[:, 0:LANES], y0, m, s1))
  assert len(out) == n_items
  return out


def _rwkv_kernel(r_ref, lw_ref, k_ref, v_ref, kkn_ref, b_ref, g_ref, rk_ref, lng_ref, lnb_ref,
                 o_ref, s_ref, y_ref):
  j = pl.program_id(1)
  pairs = RW_WIDTH // LANES

  @pl.when(j == 0)
  def _():
    s_ref[...] = jnp.zeros_like(s_ref)

  tc = r_ref.shape[1]
  rr = _iota((CHUNK, CHUNK), 0)
  cc = _iota((CHUNK, CHUNK), 1)
  tri = jnp.where(rr >= cc, 1.0, 0.0).astype(BF16)
  items = []
  for ci in range(tc // CHUNK):
    rows = slice(ci * CHUNK, (ci + 1) * CHUNK)
    lw = lw_ref[0, rows, :]
    lp = _mm(tri, lw, pa=1, pb=3)
    e_in = jnp.exp(lp)
    e_ex = jnp.exp(lp - lw)
    e_neg = jnp.exp(-lp)
    at = -kkn_ref[0, rows, :].astype(F32) * e_ex
    rt = r_ref[0, rows, :].astype(F32) * e_in
    bt = b_ref[0, rows, :].astype(F32) * e_neg
    kt = k_ref[0, rows, :].astype(F32) * e_neg
    v = v_ref[0, rows, :].astype(F32)
    p_end = e_in[CHUNK - 1:CHUNK, :]
    for p in range(pairs):
      ln = slice(p * LANES, (p + 1) * LANES)
      items.append((at[:, ln], rt[:, ln], bt[:, ln], kt[:, ln], v[:, ln], p_end[:, ln]))
  pre = _rwkv_phase1(items, RW_PASSES_INV, RW_PASSES_SCORE, RW_PASSES_MID)

  for ci in range(tc // CHUNK):
    rows = slice(ci * CHUNK, (ci + 1) * CHUNK)
    for p in range(pairs):
      rw, y0, m, s1 = pre[ci * pairs + p]
      s = s_ref[p]
      y_st = _mm(rw, s, _NT, pa=RW_PASSES_STATE, pb=RW_PASSES_STATE) + y0
      y_ref[rows, p * LANES:(p + 1) * LANES] = y_st[0:CHUNK] + y_st[CHUNK:2 * CHUNK]
      s_ref[p] = _mm(s, m, pa=RW_PASSES_STATE, pb=RW_PASSES_STATE) + s1

  y = y_ref[...]
  bd = _block_ones(RW_HEAD_DIM)
  inv_d = 1.0 / RW_HEAD_DIM
  mean = _seg_sum(y, bd) * inv_d
  d = y - mean
  var = _seg_sum(d * d, bd) * inv_d
  yn = d * lax.rsqrt(var + RW_GN_EPS) * lng_ref[...] + lnb_ref[...]
  bonus = _seg_sum(r_ref[0].astype(F32) * k_ref[0].astype(F32) * rk_ref[...], bd) * v_ref[0].astype(F32)
  o_ref[0] = ((yn + bonus) * g_ref[0].astype(F32)).astype(BF16)


def _rwkv(r, lw, k, v, kkn, bvec, g, ws, l, tc):
  b, t, _ = r.shape
  row = lambda i, j: (i, j, 0)
  seq = pl.BlockSpec((1, tc, RW_WIDTH), row)
  rk, lng, lnb = ws["rk"], ws["lng"], ws["lnb"]
  return pl.pallas_call(
      _rwkv_kernel,
      out_shape=jax.ShapeDtypeStruct((b, t, RW_WIDTH), BF16),
      grid=(b, t // tc),
      in_specs=[seq] * 7 + [_layer_spec(a, l) for a in (rk, lng, lnb)],
      out_specs=seq,
      scratch_shapes=[pltpu.VMEM((RW_WIDTH // LANES, LANES, LANES), F32),
                      pltpu.VMEM((tc, RW_WIDTH), F32)],
      compiler_params=pltpu.CompilerParams(
          dimension_semantics=("arbitrary", "arbitrary"), vmem_limit_bytes=VMEM_LIMIT_BYTES),
      name="rwkv",
  )(r, lw, k, v, kkn, bvec, g, rk, lng, lnb)


def _mla_kernel(fixed_ref, q_ref, k_ref, v_ref, o_ref, vt_ref, p_ref, acc_ref, *, tq, tk):
  i = pl.program_id(1)
  lo = _iota((tq, LANES), 1) < MLA_V_DIM
  vlo = _iota((tk, LANES), 1) < MLA_V_DIM
  nfull = i * (tq // tk)

  def score(jb, h):
    ks = pl.multiple_of(jb * tk, tk)
    q = q_ref[0, :, h * MLA_SLOT:(h + 1) * MLA_SLOT]
    return _dot(q, k_ref[0, pl.ds(ks, tk), h * MLA_SLOT:(h + 1) * MLA_SLOT], _NT)

  def value_tile(jb, h):
    ks = pl.multiple_of(jb * tk, tk)
    vb = v_ref[0, pl.ds(ks, tk), (h // 2) * LANES:(h // 2 + 1) * LANES]
    one = jnp.ones_like(vb)
    return jnp.where(vlo, vb, one) if h % 2 == 0 else jnp.where(vlo, one, vb)

  def visible(jb):
    qc = (i * tq + _iota((tq, tk), 0)) >> CHUNK_SHIFT
    kc = (jb * tk + _iota((tq, tk), 1)) >> CHUNK_SHIFT
    return kc <= qc

  def store_probs(jb):
    ks = pl.multiple_of(jb * tk, tk)
    for h in range(MLA_HEADS):
      cols = slice(h * MLA_SLOT, (h + 1) * MLA_SLOT)
      p_ref[h] = jnp.exp(_dot(k_ref[0, pl.ds(ks, tk), cols], q_ref[0, :, cols], _NT)).astype(BF16)

  def value_rows(jb, h):
    ks = pl.multiple_of(jb * tk, tk)
    vt = vt_ref[h // 2, :, pl.ds(ks, tk)]
    one = jnp.ones((MLA_V_DIM, tk), BF16)
    return (jnp.concatenate([vt[0:MLA_V_DIM], one], axis=0) if h % 2 == 0
            else jnp.concatenate([one, vt[MLA_V_DIM:2 * MLA_V_DIM]], axis=0))

  def fixed_block(jb, carry, last):
    for h in range(MLA_HEADS):
      pr = p_ref[h]
      if last:
        kc = (jb * tk + _iota((tk, tq), 0)) >> CHUNK_SHIFT
        qc = (i * tq + _iota((tk, tq), 1)) >> CHUNK_SHIFT
        pr = jnp.where(kc <= qc, pr.astype(F32), 0.0).astype(BF16)
      acc_ref[h] += _dot(value_rows(jb, h), pr)
    if not last:
      store_probs(jb + 1)
    return carry

  def finish_fixed():
    half = MLA_V_DIM
    for p in range(MLA_HEADS // 2):
      a, b = acc_ref[2 * p], acc_ref[2 * p + 1]
      out_t = jnp.concatenate([a[0:half] / a[half:2 * half], b[half:2 * half] / b[0:half]], axis=0)
      o_ref[0, :, p * LANES:(p + 1) * LANES] = out_t.T.astype(BF16)

  def online_block(jb, carry, masked):
    ms, accs = carry
    new_m, new_acc = [], []
    for h in range(MLA_HEADS):
      s, tile = score(jb, h), value_tile(jb, h)
      if masked:
        s = jnp.where(visible(jb), s, NEG_INF)
      m_new = jnp.maximum(ms[h], jnp.max(s, axis=-1, keepdims=True))
      pr = jnp.exp(s - m_new)
      new_m.append(m_new)
      new_acc.append(jnp.exp(ms[h] - m_new) * accs[h] + _dot(pr.astype(BF16), tile))
    return tuple(new_m), tuple(new_acc)

  def finish(accs):
    for p in range(MLA_HEADS // 2):
      a, b = accs[2 * p], accs[2 * p + 1]
      out = jnp.where(lo, a / pltpu.roll(a, MLA_V_DIM, 1), b / pltpu.roll(b, MLA_V_DIM, 1))
      o_ref[0, :, p * LANES:(p + 1) * LANES] = out.astype(BF16)

  fixed = fixed_ref[0, 0] == 1

  @pl.when(fixed & (i == 0))
  def _():
    for p in range(MLA_HEADS // 2):
      for c0 in range(0, v_ref.shape[1], tk):
        vt_ref[p, :, c0:c0 + tk] = v_ref[0, c0:c0 + tk, p * LANES:(p + 1) * LANES].astype(F32).T.astype(BF16)

  @pl.when(fixed)
  def _():
    acc_ref[...] = jnp.zeros_like(acc_ref)
    store_probs(0)
    lax.fori_loop(0, nfull, functools.partial(fixed_block, last=False), 0)
    fixed_block(nfull, 0, True)
    finish_fixed()

  @pl.when(jnp.logical_not(fixed))
  def _():
    zeros = tuple(jnp.zeros((tq, LANES), F32) for _ in range(MLA_HEADS))
    carry = (tuple(jnp.full((tq, 1), NEG_INF, F32) for _ in range(MLA_HEADS)), zeros)
    carry = lax.fori_loop(0, nfull, functools.partial(online_block, masked=False), carry)
    finish(online_block(nfull, carry, True)[1])


def _mla(fixed_shift, q, k, v, tq, tk):
  b, t, _ = q.shape
  assert tq == tk, "one masked diagonal block per query tile"
  return pl.pallas_call(
      functools.partial(_mla_kernel, tq=tq, tk=tk),
      out_shape=jax.ShapeDtypeStruct((b, t, MLA_WIDTH), BF16),
      grid=(b, t // tq),
      in_specs=[pl.BlockSpec(memory_space=pltpu.SMEM),
                pl.BlockSpec((1, tq, MLA_HEADS * MLA_SLOT), lambda bi, i: (bi, i, 0)),
                pl.BlockSpec((1, t, MLA_HEADS * MLA_SLOT), lambda bi, i: (bi, 0, 0)),
                pl.BlockSpec((1, t, MLA_WIDTH), lambda bi, i: (bi, 0, 0))],
      out_specs=pl.BlockSpec((1, tq, MLA_WIDTH), lambda bi, i: (bi, i, 0)),
      scratch_shapes=[pltpu.VMEM((MLA_HEADS // 2, LANES, t), BF16),
                      pltpu.VMEM((MLA_HEADS, tk, tq), BF16),
                      pltpu.VMEM((MLA_HEADS, LANES, tq), F32)],
      compiler_params=pltpu.CompilerParams(
          dimension_semantics=("arbitrary", "arbitrary"), vmem_limit_bytes=VMEM_LIMIT_BYTES),
      name="mla_attention",
  )(fixed_shift, q, k, v)


SB_STATIC_BLOCKS = 3
SB_CUMSUM_PARTS = 1


def _sb_kernel(q_ref, k_ref, v_ref, o_ref, *, tq, tk):
  for u in range(tq // tk):
    _sb_subtile(q_ref, k_ref, v_ref, o_ref, pl.program_id(1) * (tq // tk) + u, slice(u * tk, (u + 1) * tk), tk)


def _sb_subtile(q_ref, k_ref, v_ref, o_ref, i, rows, tk):
  tq = tk
  pairs = SB_HEADS // 2
  lo = _iota((tq, LANES), 1) < SB_HEAD_DIM
  vlo = _iota((tk, LANES), 1) < SB_HEAD_DIM
  rr = _iota((tk, tk + LANES), 0)
  cc = _iota((tk, tk + LANES), 1)
  tri = jnp.where((rr >= cc) | (cc >= tk), 1.0, 0.0).astype(BF16)

  def sweep(jbs, runs, accs, guard):
    qpos = i * tq + _iota((tq, tk), 0)
    stricts, kbs, vcats = [], [], []
    for jb in jbs:
      ks = pl.multiple_of(jnp.maximum(jb, 0) * tk, tk)
      kpos = jb * tk + _iota((tq, tk), 1)
      strict = kpos < qpos
      stricts.append(strict & (kpos >= 0) if guard else strict)
      kbs.append([k_ref[0, pl.ds(ks, tk), p * LANES:(p + 1) * LANES] for p in range(pairs)])
      vc = []
      for p in range(pairs):
        vb = v_ref[0, pl.ds(ks, tk), p * LANES:(p + 1) * LANES]
        vzero = jnp.zeros_like(vb)
        vc.append(jnp.concatenate([jnp.where(vlo, vb, vzero), jnp.where(vlo, vzero, vb)], axis=0))
      vcats.append(vc)
    qs = []
    for p in range(pairs):
      qp = q_ref[0, rows, p * LANES:(p + 1) * LANES]
      qzero = jnp.zeros_like(qp)
      qs += [jnp.where(lo, qp, qzero), jnp.where(lo, qzero, qp)]
    heads = range(SB_HEADS)
    z = [[_dot(qs[h], kbs[d][h // 2], _NT) for h in heads] for d in range(len(jbs))]
    ls = [[jnp.where(stricts[d], -_softplus(z[d][h]), 0.0) for h in heads] for d in range(len(jbs))]
    cs = [[_mm(ls[d][h], tri, pa=SB_CUMSUM_PARTS, pb=1) for h in heads] for d in range(len(jbs))]
    runs = list(runs)
    weights = [[] for _ in range(pairs)]
    for d in range(len(jbs)):
      for h in heads:
        logw = z[d][h] + cs[d][h][:, 0:tk] + jnp.concatenate([runs[h]] * (tk // LANES), axis=-1)
        weights[h // 2].append(jnp.where(stricts[d], jnp.exp(logw), 0.0).astype(BF16))
        runs[h] = runs[h] + cs[d][h][:, tk:tk + LANES]
    new_accs = []
    for p in range(pairs):
      vall = jnp.concatenate([vcats[d][p] for d in range(len(jbs))], axis=0)
      new_accs.append(accs[p] + _dot(jnp.concatenate(weights[p], axis=1), vall))
    return tuple(runs), tuple(new_accs)

  zr = jnp.zeros((tq, LANES), F32)
  start = (i + 1) * (tq // tk) - 1
  runs, accs = sweep([start - d for d in range(SB_STATIC_BLOCKS)], (zr,) * SB_HEADS, (zr,) * pairs, True)

  def cond(state):
    jb, runs, _ = state
    top = functools.reduce(jnp.maximum, runs)
    return (jb >= 0) & (jnp.max(top) > SB_LOG_FLOOR)

  def body(state):
    jb, runs, accs = state
    runs, accs = sweep([jb], runs, accs, False)
    return jb - 1, runs, accs

  _, _, accs = lax.while_loop(cond, body, (start - SB_STATIC_BLOCKS, runs, accs))
  for p in range(pairs):
    o_ref[0, rows, p * LANES:(p + 1) * LANES] = accs[p].astype(BF16)


def _sb(q, k, v, tq, tk):
  b, t, _ = q.shape
  return pl.pallas_call(
      functools.partial(_sb_kernel, tq=tq, tk=tk),
      out_shape=jax.ShapeDtypeStruct((b, t, SB_WIDTH), BF16),
      grid=(b, t // tq),
      in_specs=[pl.BlockSpec((1, tq, SB_WIDTH), lambda bi, i: (bi, i, 0)),
                pl.BlockSpec((1, t, SB_WIDTH), lambda bi, i: (bi, 0, 0)),
                pl.BlockSpec((1, t, SB_WIDTH), lambda bi, i: (bi, 0, 0))],
      out_specs=pl.BlockSpec((1, tq, SB_WIDTH), lambda bi, i: (bi, i, 0)),
      compiler_params=pltpu.CompilerParams(
          dimension_semantics=("arbitrary", "arbitrary"), vmem_limit_bytes=VMEM_LIMIT_BYTES),
      name="sb_attention",
  )(q, k, v)


FFN_CHUNKS = (1024, 1024, 768)
assert sum(FFN_CHUNKS) == FFN_HIDDEN


def _outffn_kernel(x_ref, yrw_ref, ymla_ref, ysb_ref, wo_ref, ng_ref, wg_ref, wu_ref, wd_ref, o_ref):
  mix = (_dot(yrw_ref[...], wo_ref[0:RW_WIDTH, :])
         + _dot(ymla_ref[...], wo_ref[RW_WIDTH:RW_WIDTH + MLA_WIDTH, :])
         + _dot(ysb_ref[...], wo_ref[RW_WIDTH + MLA_WIDTH:, :]))
  x1 = x_ref[...] + mix
  hn = (x1 * _rms_scale(x1, D_MODEL) * ng_ref[...]).astype(BF16)
  acc = x1
  start = 0
  for width in FFN_CHUNKS:
    cols = slice(start, start + width)
    gate = _dot(hn, wg_ref[:, cols])
    up = _dot(hn, wu_ref[:, cols])
    act = (gate * _sigmoid(gate) * up).astype(BF16)
    acc = acc + _dot(act, wd_ref[cols, :])
    start += width
  o_ref[...] = acc


def _outffn(x, yrw, ymla, ysb, ws, l, tm):
  n = x.shape[0]
  row = lambda i: (i, 0)
  resident = lambda a: _layer_spec(a, l, pipeline_mode=pl.Buffered(1))
  weights = [ws[k] for k in ("wo", "fg", "wg", "wu", "wd")]
  return pl.pallas_call(
      _outffn_kernel,
      out_shape=jax.ShapeDtypeStruct((n, D_MODEL), F32),
      grid=(n // tm,),
      in_specs=[pl.BlockSpec((tm, D_MODEL), row),
                pl.BlockSpec((tm, RW_WIDTH), row),
                pl.BlockSpec((tm, MLA_WIDTH), row),
                pl.BlockSpec((tm, SB_WIDTH), row)] + [resident(w) for w in weights],
      out_specs=pl.BlockSpec((tm, D_MODEL), row),
      compiler_params=pltpu.CompilerParams(
          dimension_semantics=("arbitrary",), vmem_limit_bytes=VMEM_LIMIT_BYTES),
      name="outproj_ffn",
  )(x, yrw, ymla, ysb, *weights)


def _pad_heads(w, heads, dim, slot):
  lead = w.shape[:-1]
  w = w.reshape(lead + (heads, dim))
  w = jnp.pad(w, [(0, 0)] * len(lead) + [(0, 0), (0, slot - dim)])
  return w.reshape(lead + (heads * slot,))


def _stack_weights(attn_norm_g, w_in, rw_shift_mu, rw_w_up, rw_w0, rw_a_up, rw_a0, rw_g_up, rw_k_k,
                   rw_k_a, rw_r_k, rw_ln_g, rw_ln_b, mla_cq_norm_g, mla_ckv_norm_g, mla_w_uq, mla_w_ukv,
                   mla_q_norm_g, mla_k_norm_g, w_o, ffn_norm_g, ffn_w_gate, ffn_w_up, ffn_w_down):
  depth = w_in.shape[0]
  vec = lambda a: a.reshape(depth, 1, -1)
  o = COL_CKV + MLA_KV_RANK
  assert o == COL_SBQ
  rope_slot = jnp.pad(w_in[:, :, o:o + MLA_ROPE_DIM],
                      ((0, 0), (0, 0), (MLA_NOPE_DIM, LANES - MLA_NOPE_DIM - MLA_ROPE_DIM)))
  win_p = w_in[:, :, :COL_CKV].astype(BF16)
  win_tail = jnp.concatenate([w_in[:, :, COL_CKV:o], w_in[:, :, o + MLA_ROPE_DIM:], rope_slot], axis=2).astype(BF16)
  ukv = mla_w_ukv.reshape(depth, MLA_KV_RANK, MLA_HEADS, MLA_NOPE_DIM + MLA_V_DIM)
  wuk = _pad_heads(ukv[..., :MLA_NOPE_DIM].reshape(depth, MLA_KV_RANK, -1), MLA_HEADS, MLA_NOPE_DIM, MLA_SLOT)
  wuv = ukv[..., MLA_NOPE_DIM:].reshape(depth, MLA_KV_RANK, MLA_HEADS * MLA_V_DIM)
  head_gain = lambda g: jnp.tile(jnp.pad(g, ((0, 0), (0, MLA_SLOT - MLA_QK_DIM))), (1, MLA_HEADS)).reshape(depth, 1, -1)
  bound = (MLA_QK_DIM ** 0.5) * jnp.max(jnp.abs(mla_q_norm_g), axis=1) * jnp.max(jnp.abs(mla_k_norm_g), axis=1)
  shift = (MLA_BOUND_MARGIN * bound).reshape(depth, 1, 1)
  fixed_shift = shift <= MLA_MAX_SHIFT
  bias_lane = jnp.tile(jnp.arange(MLA_SLOT) == MLA_QK_DIM, MLA_HEADS).reshape(1, 1, -1)
  qb = jnp.where(bias_lane & fixed_shift, -shift, 0.0).astype(F32)
  kb = jnp.broadcast_to(jnp.where(bias_lane, 1.0, 0.0).astype(F32), qb.shape)
  return dict(
      qb=qb, kb=kb, fixed_shift=fixed_shift.astype(jnp.int32),
      ng=vec(attn_norm_g), win=win_p, wtail=win_tail, mu=vec(rw_shift_mu), wup=rw_w_up, w0=vec(rw_w0),
      aup=rw_a_up, a0=vec(rw_a0), gup=rw_g_up, kk=vec(rw_k_k), ka=vec(rw_k_a),
      rk=vec(rw_r_k), lng=vec(rw_ln_g), lnb=vec(rw_ln_b),
      cqg=vec(mla_cq_norm_g), ckvg=vec(mla_ckv_norm_g),
      wuq=_pad_heads(mla_w_uq, MLA_HEADS, MLA_QK_DIM, MLA_SLOT).astype(BF16),
      wuk=wuk.astype(BF16), wuv=wuv.astype(BF16),
      qg=head_gain(mla_q_norm_g), kg=head_gain(mla_k_norm_g),
      wo=w_o.astype(BF16), fg=vec(ffn_norm_g),
      wg=ffn_w_gate.astype(BF16), wu=ffn_w_up.astype(BF16), wd=ffn_w_down.astype(BF16))


def _layer_spec(a, l, **kwargs):
  zeros = (0,) * (a.ndim - 1)
  return pl.BlockSpec((None,) + a.shape[1:], lambda *_: (l,) + zeros, **kwargs)


def kernel(x, positions, attn_norm_g, w_in, rw_shift_mu, rw_w_up, rw_w0, rw_a_up, rw_a0, rw_g_up, rw_k_k, rw_k_a, rw_r_k, rw_ln_g, rw_ln_b, mla_cq_norm_g, mla_ckv_norm_g, mla_w_uq, mla_w_ukv, mla_q_norm_g, mla_k_norm_g, w_o, ffn_norm_g, ffn_w_gate, ffn_w_up, ffn_w_down):
  b, t, d = x.shape
  depth = w_in.shape[0]
  params = (attn_norm_g, w_in, rw_shift_mu, rw_w_up, rw_w0, rw_a_up, rw_a0, rw_g_up, rw_k_k, rw_k_a, rw_r_k,
            rw_ln_g, rw_ln_b, mla_cq_norm_g, mla_ckv_norm_g, mla_w_uq, mla_w_ukv, mla_q_norm_g, mla_k_norm_g,
            w_o, ffn_norm_g, ffn_w_gate, ffn_w_up, ffn_w_down)
  tm = min(512, t)
  cos, sin = _rope_tables(positions, tm)
  ws = _stack_weights(*params)
  for l in range(depth):
    (r, lwd, k, v, kkn, bvec, g, mq, mk, mv, sq, sk, sv) = _inproj(x, cos, sin, ws, l, tm)
    y_rw = _rwkv(r, lwd, k, v, kkn, bvec, g, ws, l, tc=min(512, t))
    y_mla = _mla(ws["fixed_shift"][l], mq, mk, mv, tq=min(512, t), tk=min(512, t))
    y_sb = _sb(sq, sk, sv, tq=min(512, t), tk=LANES)
    x = _outffn(x.reshape(b * t, d), y_rw.reshape(b * t, -1), y_mla.reshape(b * t, -1),
                y_sb.reshape(b * t, -1), ws, l, tm=512).reshape(b, t, d)
  return x
```

```python
import functools

import jax
import jax.numpy as jnp
from jax import lax
from jax.experimental import pallas as pl
from jax.experimental.pallas import tpu as pltpu

F32 = jnp.float32
BF16 = jnp.bfloat16

LANES = 128
VMEM_LIMIT_BYTES = 56 * 1024 * 1024

TOKEN_TILE = 512
SB_TILE = LANES

D_MODEL = 1024
CHUNK = 64
CHUNK_SHIFT = 6
RW_HEADS = 8
RW_HEAD_DIM = 64
RW_WIDTH = RW_HEADS * RW_HEAD_DIM
RW_DECAY_LORA = 64
RW_AAA_LORA = 64
RW_GATE_LORA = 128
RW_COLS = 3 * RW_WIDTH + RW_DECAY_LORA + RW_AAA_LORA + RW_GATE_LORA
RW_GN_EPS = 64e-5
MLA_HEADS = 4
MLA_NOPE_DIM = 64
MLA_ROPE_DIM = 32
MLA_V_DIM = 64
MLA_QK_DIM = MLA_NOPE_DIM + MLA_ROPE_DIM
MLA_Q_RANK = 256
MLA_KV_RANK = 128
MLA_WIDTH = MLA_HEADS * MLA_V_DIM
MLA_SLOT = LANES
SB_HEADS = 4
SB_HEAD_DIM = 64
SB_WIDTH = SB_HEADS * SB_HEAD_DIM
FFN_HIDDEN = 2816
ROPE_THETA = 10000.0
NORM_EPS = 1e-6
NEG_INF = -1e30
MLA_BOUND_MARGIN = 1.03
MLA_MAX_SHIFT = 41.0
SB_LOG_FLOOR = -110.0

COL_RW = 0
COL_CQ = RW_COLS
COL_CKV = COL_CQ + MLA_Q_RANK
COL_SBQ = COL_CKV + MLA_KV_RANK
COL_SBK = COL_SBQ + SB_WIDTH
COL_SBV = COL_SBK + SB_WIDTH
COL_ROPE = COL_SBV + SB_WIDTH
IN_COLS_PAD = COL_ROPE + LANES


def _split_bf16(x, parts):
  out = []
  rem = x
  for i in range(parts):
    p = rem.astype(BF16)
    out.append(p)
    if i + 1 < parts:
      rem = rem - p.astype(F32)
  return out


_NN = (((1,), (0,)), ((), ()))
_NT = (((1,), (1,)), ((), ()))
_TN = (((0,), (0,)), ((), ()))


def _dot(a, b, dims=_NN):
  return lax.dot_general(a, b, dims, preferred_element_type=F32)


def _mm(a, b, dims=_NN, pa=2, pb=2):
  as_ = _split_bf16(a, pa) if a.dtype != BF16 else [a]
  bs_ = _split_bf16(b, pb) if b.dtype != BF16 else [b]
  order = max(len(as_), len(bs_))
  acc = None
  for i, ai in enumerate(as_):
    for j, bj in enumerate(bs_):
      if i + j < order:
        t = _dot(ai, bj, dims)
        acc = t if acc is None else acc + t
  return acc


def _sigmoid(x):
  return 1.0 / (1.0 + jnp.exp(-x))


def _softplus(x):
  return jnp.maximum(x, 0.0) + jnp.log(1.0 + jnp.exp(-jnp.abs(x)))


def _iota(shape, dim):
  return lax.broadcasted_iota(jnp.int32, shape, dim)


def _block_ones(block):
  shift = block.bit_length() - 1
  r = _iota((LANES, LANES), 0) >> shift
  c = _iota((LANES, LANES), 1) >> shift
  return jnp.where(r == c, 1.0, 0.0).astype(BF16)


SEG_SUM_PARTS = 1
LORA_PARTS = 1


def _seg_sum(x, bd, fill=None):
  groups = []
  for g in range(0, x.shape[1], LANES):
    groups.append(_mm(x[:, g:g + LANES], bd, pa=SEG_SUM_PARTS, pb=1))
    if fill is not None and g % (2 * LANES) == 0:
      fill()
  return jnp.concatenate(groups, axis=1)


def _rope_kernel(pos_ref, invf_ref, cos_ref, sin_ref):
  ang = pos_ref[0].astype(F32) * invf_ref[...]
  lane = _iota(ang.shape, 1)
  first_half = lane < MLA_NOPE_DIM + MLA_ROPE_DIM // 2
  cos_ref[0] = jnp.cos(ang)
  sin_ref[0] = jnp.where(first_half, -jnp.sin(ang), jnp.sin(ang))


def _rope_tables(positions, tm):
  b, t = positions.shape
  inv_freq = ROPE_THETA ** (-jnp.arange(0, MLA_ROPE_DIM, 2, dtype=F32) / MLA_ROPE_DIM)
  invf = jnp.zeros((1, MLA_SLOT), F32)
  invf = invf.at[0, MLA_NOPE_DIM:MLA_NOPE_DIM + MLA_ROPE_DIM].set(jnp.concatenate([inv_freq, inv_freq]))
  out = jax.ShapeDtypeStruct((b, t, MLA_SLOT), F32)
  return pl.pallas_call(
      _rope_kernel,
      out_shape=(out, out),
      grid=(b, t // tm),
      in_specs=[pl.BlockSpec((1, tm, 1), lambda i, j: (i, j, 0)),
                pl.BlockSpec((1, MLA_SLOT), lambda i, j: (0, 0))],
      out_specs=(pl.BlockSpec((1, tm, MLA_SLOT), lambda i, j: (i, j, 0)),
                 pl.BlockSpec((1, tm, MLA_SLOT), lambda i, j: (i, j, 0))),
      name="rope_tables",
  )(positions.reshape(b, t, 1), invf)


INPROJ_SUBTILE = 256
INPROJ_SECTION = 256


def _rms_scale(x, width):
  ms = jnp.sum(x * x, axis=-1, keepdims=True) * (1.0 / width)
  return lax.rsqrt(ms + NORM_EPS)


def _rotary(x, cos, sin):
  half = MLA_ROPE_DIM // 2
  outs = []
  for h in range(MLA_HEADS):
    xh = x[:, h * MLA_SLOT:(h + 1) * MLA_SLOT]
    lane = _iota(xh.shape, 1)
    up = pltpu.roll(xh, half, 1)
    down = pltpu.roll(xh, MLA_SLOT - half, 1)
    partner = jnp.where(lane >= MLA_NOPE_DIM + half, up, down)
    outs.append(xh * cos + partner * sin)
  return jnp.concatenate(outs, axis=-1)


def _head_rms(x, gain, bd, fill):
  ms = _seg_sum(x * x, bd, fill) * (1.0 / MLA_QK_DIM)
  return x * lax.rsqrt(ms + NORM_EPS) * gain


def _inproj_kernel(x_ref, cos_ref, sin_ref, ng_ref, win_ref, mu_ref, wup_ref, w0_ref, aup_ref, a0_ref,
                   gup_ref, kk_ref, ka_ref, cqg_ref, ckvg_ref, wuq_ref, wuk_ref, wuv_ref, qg_ref, kg_ref,
                   qb_ref, kb_ref,
                   r_out, lw_out, k_out, v_out, kkn_out, b_out, g_out,
                   mq_out, mk_out, mv_out, sq_out, sk_out, sv_out,
                   carry_ref, ha_ref, hb_ref, *, tiles_per_row):
  j = pl.program_id(0)

  @pl.when(j == 0)
  def _():
    hb_ref[...] = jnp.zeros_like(hb_ref)
    carry_ref[...] = jnp.zeros_like(carry_ref)

  tm = x_ref.shape[0]
  sub = min(INPROJ_SUBTILE, tm)
  restart = (lax.rem(j + tiles_per_row - 1, tiles_per_row) == 0) | (j == 0)
  out_refs = (r_out, lw_out, k_out, v_out, kkn_out, b_out, g_out, mq_out, mk_out, mv_out, sq_out, sk_out, sv_out)

  def step(h_new, h_old):
    last_row = jnp.where(restart, 0.0, carry_ref[...])
    for r0 in range(0, tm, sub):
      rows = slice(r0, r0 + sub)
      x = x_ref[rows, :]
      xn = (x * _rms_scale(x, D_MODEL) * ng_ref[...]).astype(BF16)

      def sections(xn=xn, rows=rows):
        for c0 in range(0, IN_COLS_PAD, INPROJ_SECTION):
          h_new[rows, c0:c0 + INPROJ_SECTION] = _dot(xn, win_ref[:, c0:c0 + INPROJ_SECTION])
          yield
      fill = sections()
      last_row = _inproj_rows(h_old.at[rows, :], cos_ref[rows, :], sin_ref[rows, :], last_row,
                              mu_ref, wup_ref, w0_ref, aup_ref, a0_ref, gup_ref, kk_ref, ka_ref,
                              cqg_ref, ckvg_ref, wuq_ref, wuk_ref, wuv_ref, qg_ref, kg_ref, qb_ref, kb_ref,
                              tuple(o.at[rows, :] for o in out_refs), lambda fill=fill: next(fill, None))
      for _ in fill:
        pass
    carry_ref[...] = last_row

  @pl.when(lax.rem(j, 2) == 0)
  def _():
    step(ha_ref, hb_ref)

  @pl.when(lax.rem(j, 2) == 1)
  def _():
    step(hb_ref, ha_ref)


def _inproj_rows(h_ref, cos, sin, prev_row, mu_ref, wup_ref, w0_ref, aup_ref, a0_ref, gup_ref,
                 kk_ref, ka_ref, cqg_ref, ckvg_ref, wuq_ref, wuk_ref, wuv_ref, qg_ref, kg_ref, qb_ref, kb_ref, outs,
                 fill):
  (r_out, lw_out, k_out, v_out, kkn_out, b_out, g_out, mq_out, mk_out, mv_out, sq_out, sk_out, sv_out) = outs
  n = h_ref.shape[0]

  fill()
  h_rw = h_ref[:, COL_RW:COL_RW + RW_COLS]
  row = _iota(h_rw.shape, 0)
  prev = jnp.where(row == 0, prev_row, pltpu.roll(h_rw, 1, 0))
  sh = h_rw + mu_ref[...] * (prev - h_rw)
  r = sh[:, 0:RW_WIDTH]
  k = sh[:, RW_WIDTH:2 * RW_WIDTH]
  v = sh[:, 2 * RW_WIDTH:3 * RW_WIDTH]
  o = 3 * RW_WIDTH
  wd = sh[:, o:o + RW_DECAY_LORA]
  ad = sh[:, o + RW_DECAY_LORA:o + RW_DECAY_LORA + RW_AAA_LORA]
  gd = sh[:, o + RW_DECAY_LORA + RW_AAA_LORA:RW_COLS]

  lora = functools.partial(_mm, pa=LORA_PARTS, pb=LORA_PARTS)
  w_raw = -_softplus(-(w0_ref[...] + lora(jnp.tanh(wd), wup_ref[...]))) - 0.5
  lw_out[...] = -jnp.exp(w_raw)
  fill()
  a = _sigmoid(a0_ref[...] + lora(ad, aup_ref[...]))
  g_out[...] = lora(_sigmoid(gd), gup_ref[...])
  kk = k * kk_ref[...]
  bd64 = _block_ones(RW_HEAD_DIM)
  fill()
  kkn = kk * lax.rsqrt(_seg_sum(kk * kk, bd64, fill) + 1e-12)
  r_out[...] = r
  k_out[...] = k * (1.0 + (a - 1.0) * ka_ref[...])
  v_out[...] = v
  kkn_out[...] = kkn
  b_out[...] = kkn * a

  c_q = h_ref[:, COL_CQ:COL_CQ + MLA_Q_RANK]
  c_kv = h_ref[:, COL_CKV:COL_CKV + MLA_KV_RANK]
  k_rope = h_ref[:, COL_ROPE:COL_ROPE + LANES]
  cqn = (c_q * _rms_scale(c_q, MLA_Q_RANK) * cqg_ref[...]).astype(BF16)
  ckvn = (c_kv * _rms_scale(c_kv, MLA_KV_RANK) * ckvg_ref[...]).astype(BF16)
  bd128 = _block_ones(MLA_SLOT)
  fill()
  q = _head_rms(_dot(cqn, wuq_ref[...]), qg_ref[...], bd128, fill)
  fill()
  kf = _dot(ckvn, wuk_ref[...]) + jnp.concatenate([k_rope] * MLA_HEADS, axis=-1)
  kf = _head_rms(kf, kg_ref[...], bd128, fill)
  mq_out[...] = (_rotary(q, cos, sin) * (MLA_QK_DIM ** -0.5) + qb_ref[...]).astype(BF16)
  fill()
  mk_out[...] = (_rotary(kf, cos, sin) + kb_ref[...]).astype(BF16)
  mv_out[...] = _dot(ckvn, wuv_ref[...]).astype(BF16)

  sq_out[...] = (h_ref[:, COL_SBQ:COL_SBQ + SB_WIDTH] * (SB_HEAD_DIM ** -0.5)).astype(BF16)
  sk_out[...] = h_ref[:, COL_SBK:COL_SBK + SB_WIDTH].astype(BF16)
  sv_out[...] = h_ref[:, COL_SBV:COL_SBV + SB_WIDTH].astype(BF16)
  return h_rw[n - 1:n, :]


def _inproj(x, cos, sin, ws, l, tm):
  b, t, _ = x.shape
  n_tiles = b * t // tm
  cur = lambda j: (jnp.minimum(j, n_tiles - 1), 0)
  done = lambda j: (jnp.maximum(j - 1, 0), 0)
  full = lambda a: _layer_spec(a, l)
  weights = [ws[n] for n in ("ng", "win", "mu", "wup", "w0", "aup", "a0", "gup", "kk", "ka",
                             "cqg", "ckvg", "wuq", "wuk", "wuv", "qg", "kg", "qb", "kb")]
  rw = jax.ShapeDtypeStruct((b * t, RW_WIDTH), F32)
  att4 = jax.ShapeDtypeStruct((b * t, MLA_HEADS * MLA_SLOT), BF16)
  att2 = jax.ShapeDtypeStruct((b * t, SB_WIDTH), BF16)
  out_shape = (rw,) * 7 + (att4, att4, att2, att2, att2, att2)
  out_specs = tuple(pl.BlockSpec((tm, s.shape[-1]), done) for s in out_shape)
  outs = pl.pallas_call(
      functools.partial(_inproj_kernel, tiles_per_row=t // tm),
      out_shape=out_shape,
      grid=(n_tiles + 1,),
      in_specs=[pl.BlockSpec((tm, D_MODEL), cur),
                pl.BlockSpec((tm, MLA_SLOT), done),
                pl.BlockSpec((tm, MLA_SLOT), done)] + [full(w) for w in weights],
      out_specs=out_specs,
      scratch_shapes=[pltpu.VMEM((1, RW_COLS), F32),
                      pltpu.VMEM((tm, IN_COLS_PAD), F32),
                      pltpu.VMEM((tm, IN_COLS_PAD), F32)],
      compiler_params=pltpu.CompilerParams(
          dimension_semantics=("arbitrary",), vmem_limit_bytes=VMEM_LIMIT_BYTES),
      name="inproj",
  )(x.reshape(b * t, -1), cos.reshape(b * t, -1), sin.reshape(b * t, -1), *weights)
  return tuple(o.reshape(b, t, -1) for o in outs)


RW_PASSES_SCORE = 1
RW_PASSES_INV = 1
RW_PASSES_MID = 1
RW_PASSES_STATE = 1


def _rwkv_phase1(items, p_inv, p_score, p_mid):
  c = CHUNK
  n_items = len(items)
  lane = _iota((c, LANES), 1)
  lo = lane < RW_HEAD_DIM
  stack = lambda m: jnp.concatenate([jnp.where(lo, m, 0.0), jnp.where(lo, 0.0, m)], axis=0)
  r2 = _iota((2 * c, 2 * c), 0)
  c2 = _iota((2 * c, 2 * c), 1)
  eye = jnp.where(r2 == c2, 1.0, 0.0).astype(F32)
  r4 = _iota((4 * c, 4 * c), 0)
  c4 = _iota((4 * c, 4 * c), 1)
  keep = ((r4 & (c - 1)) + jnp.where(r4 >= 2 * c, 1, 0)) > (c4 & (c - 1))
  mmm = functools.partial(_mm, pa=p_mid, pb=p_mid)

  def both(key_a, key_b, out_key, parts, rows=None):
    for s in st:
      a = s[key_a] if rows is None else rows(s[key_a])
      s[out_key] = _mm(a, s[key_b], pa=parts, pb=parts)

  st = []
  for at, rt, bt, kt, v, p_end in items:
    a_st, r_st, b_st, k_st, v_st = stack(at), stack(rt), stack(bt), stack(kt), stack(v)
    rhs = jnp.concatenate([b_st, k_st], axis=0)
    sc = _mm(jnp.concatenate([a_st, r_st], axis=0), rhs, _NT, pa=p_score, pb=p_score)
    sc = jnp.where(keep, sc, 0.0)
    st.append(dict(a_st=a_st, r_st=r_st, v_st=v_st, rhs=rhs, p_end=p_end,
                   n=sc[0:2 * c, 0:2 * c], a_ak=sc[0:2 * c, 2 * c:4 * c], r_bk=sc[2 * c:4 * c, :]))
  for s in st:
    s["nd"] = jnp.where((r2 >> 3) == (c2 >> 3), s["n"], 0.0)
  both("nd", "nd", "n2", p_inv)
  both("n2", "n2", "n4", p_inv)
  both("nd", "n2", "n3", p_inv)
  for s in st:
    s["p1"] = eye + s["nd"] + s["n2"] + s["n3"]
  both("p1", "n4", "t", p_inv)
  for s in st:
    s["inv"] = s["p1"] + s["t"]
  shift = 3
  while (1 << shift) < c:
    bs = 1 << shift
    groups = range(2 * c // (2 * bs))
    lower = lambda m, bs=bs, groups=groups: jnp.concatenate(
        [m[g * 2 * bs + bs:(g + 1) * 2 * bs] for g in groups], axis=0)
    sel = ((r2 >> (shift + 1)) == (c2 >> (shift + 1))) & ((r2 >> shift) != (c2 >> shift))
    for s in st:
      s["off"] = jnp.where(sel, s["n"], 0.0)
    both("inv", "off", "t", p_inv, rows=lower)
    both("t", "inv", "t", p_inv)
    for s in st:
      inv, t = s["inv"], s["t"]
      pieces = []
      for g in groups:
        pieces += [inv[g * 2 * bs:g * 2 * bs + bs], inv[g * 2 * bs + bs:(g + 1) * 2 * bs] + t[g * bs:(g + 1) * bs]]
      s["inv"] = jnp.concatenate(pieces, axis=0)
    shift += 1
  both("a_ak", "v_st", "av", p_mid)
  for s in st:
    s["wu"] = mmm(s["inv"], jnp.concatenate([s["a_st"], s["av"]], axis=1))
  out = []
  for s in st:
    wt, ut = s["wu"][:, 0:LANES], s["wu"][:, LANES:2 * LANES]
    r_b, r_k = s["r_bk"][:, 0:2 * c], s["r_bk"][:, 2 * c:4 * c]
    rwy = mmm(r_b, s["wu"])
    y0 = rwy[:, LANES:2 * LANES] + mmm(r_k, s["v_st"])
    m = (eye + mmm(wt, s["rhs"][0:2 * c], _TN)) * s["p_end"]
    s1 = mmm(jnp.concatenate([ut, s["v_st"]], axis=0), s["rhs"], _TN) * s["p_end"]
    out.append((s["r_st"] + rwy[:, 0:LANES], y0, m, s1))
  assert len(out) == n_items
  return out


def _rwkv_kernel(r_ref, lw_ref, k_ref, v_ref, kkn_ref, b_ref, g_ref, rk_ref, lng_ref, lnb_ref,
                 o_ref, s_ref, y_ref):
  j = pl.program_id(1)
  pairs = RW_WIDTH // LANES

  @pl.when(j == 0)
  def _():
    s_ref[...] = jnp.zeros_like(s_ref)

  tc = r_ref.shape[1]
  rr = _iota((CHUNK, CHUNK), 0)
  cc = _iota((CHUNK, CHUNK), 1)
  tri = jnp.where(rr >= cc, 1.0, 0.0).astype(BF16)
  items = []
  for ci in range(tc // CHUNK):
    rows = slice(ci * CHUNK, (ci + 1) * CHUNK)
    lw = lw_ref[0, rows, :]
    lp = _mm(tri, lw, pa=1, pb=3)
    e_in = jnp.exp(lp)
    e_ex = jnp.exp(lp - lw)
    e_neg = jnp.exp(-lp)
    at = -kkn_ref[0, rows, :] * e_ex
    rt = r_ref[0, rows, :] * e_in
    bt = b_ref[0, rows, :] * e_neg
    kt = k_ref[0, rows, :] * e_neg
    v = v_ref[0, rows, :]
    p_end = e_in[CHUNK - 1:CHUNK, :]
    for p in range(pairs):
      ln = slice(p * LANES, (p + 1) * LANES)
      items.append((at[:, ln], rt[:, ln], bt[:, ln], kt[:, ln], v[:, ln], p_end[:, ln]))
  pre = _rwkv_phase1(items, RW_PASSES_INV, RW_PASSES_SCORE, RW_PASSES_MID)

  for ci in range(tc // CHUNK):
    rows = slice(ci * CHUNK, (ci + 1) * CHUNK)
    for p in range(pairs):
      rw, y0, m, s1 = pre[ci * pairs + p]
      s = s_ref[p]
      y_st = _mm(rw, s, _NT, pa=RW_PASSES_STATE, pb=RW_PASSES_STATE) + y0
      y_ref[rows, p * LANES:(p + 1) * LANES] = y_st[0:CHUNK] + y_st[CHUNK:2 * CHUNK]
      s_ref[p] = _mm(s, m, pa=RW_PASSES_STATE, pb=RW_PASSES_STATE) + s1

  y = y_ref[...]
  bd = _block_ones(RW_HEAD_DIM)
  inv_d = 1.0 / RW_HEAD_DIM
  mean = _seg_sum(y, bd) * inv_d
  d = y - mean
  var = _seg_sum(d * d, bd) * inv_d
  yn = d * lax.rsqrt(var + RW_GN_EPS) * lng_ref[...] + lnb_ref[...]
  bonus = _seg_sum(r_ref[0] * k_ref[0] * rk_ref[...], bd) * v_ref[0]
  o_ref[0] = ((yn + bonus) * g_ref[0]).astype(BF16)


def _rwkv(r, lw, k, v, kkn, bvec, g, ws, l, tc):
  b, t, _ = r.shape
  row = lambda i, j: (i, j, 0)
  seq = pl.BlockSpec((1, tc, RW_WIDTH), row)
  rk, lng, lnb = ws["rk"], ws["lng"], ws["lnb"]
  return pl.pallas_call(
      _rwkv_kernel,
      out_shape=jax.ShapeDtypeStruct((b, t, RW_WIDTH), BF16),
      grid=(b, t // tc),
      in_specs=[seq] * 7 + [_layer_spec(a, l) for a in (rk, lng, lnb)],
      out_specs=seq,
      scratch_shapes=[pltpu.VMEM((RW_WIDTH // LANES, LANES, LANES), F32),
                      pltpu.VMEM((tc, RW_WIDTH), F32)],
      compiler_params=pltpu.CompilerParams(
          dimension_semantics=("arbitrary", "arbitrary"), vmem_limit_bytes=VMEM_LIMIT_BYTES),
      name="rwkv",
  )(r, lw, k, v, kkn, bvec, g, rk, lng, lnb)


def _mla_kernel(fixed_ref, q_ref, k_ref, v_ref, o_ref, vt_ref, p_ref, acc_ref, *, tq, tk):
  i = pl.program_id(1)
  lo = _iota((tq, LANES), 1) < MLA_V_DIM
  vlo = _iota((tk, LANES), 1) < MLA_V_DIM
  nfull = i * (tq // tk)

  def score(jb, h):
    ks = pl.multiple_of(jb * tk, tk)
    q = q_ref[0, :, h * MLA_SLOT:(h + 1) * MLA_SLOT]
    return _dot(q, k_ref[0, pl.ds(ks, tk), h * MLA_SLOT:(h + 1) * MLA_SLOT], _NT)

  def value_tile(jb, h):
    ks = pl.multiple_of(jb * tk, tk)
    vb = v_ref[0, pl.ds(ks, tk), (h // 2) * LANES:(h // 2 + 1) * LANES]
    one = jnp.ones_like(vb)
    return jnp.where(vlo, vb, one) if h % 2 == 0 else jnp.where(vlo, one, vb)

  def visible(jb):
    qc = (i * tq + _iota((tq, tk), 0)) >> CHUNK_SHIFT
    kc = (jb * tk + _iota((tq, tk), 1)) >> CHUNK_SHIFT
    return kc <= qc

  def store_probs(jb):
    ks = pl.multiple_of(jb * tk, tk)
    for h in range(MLA_HEADS):
      cols = slice(h * MLA_SLOT, (h + 1) * MLA_SLOT)
      p_ref[h] = jnp.exp(_dot(k_ref[0, pl.ds(ks, tk), cols], q_ref[0, :, cols], _NT)).astype(BF16)

  def value_rows(jb, h):
    ks = pl.multiple_of(jb * tk, tk)
    vt = vt_ref[h // 2, :, pl.ds(ks, tk)]
    one = jnp.ones((MLA_V_DIM, tk), BF16)
    return (jnp.concatenate([vt[0:MLA_V_DIM], one], axis=0) if h % 2 == 0
            else jnp.concatenate([one, vt[MLA_V_DIM:2 * MLA_V_DIM]], axis=0))

  def fixed_block(jb, carry, last):
    for h in range(MLA_HEADS):
      pr = p_ref[h]
      if last:
        kc = (jb * tk + _iota((tk, tq), 0)) >> CHUNK_SHIFT
        qc = (i * tq + _iota((tk, tq), 1)) >> CHUNK_SHIFT
        pr = jnp.where(kc <= qc, pr.astype(F32), 0.0).astype(BF16)
      acc_ref[h] += _dot(value_rows(jb, h), pr)
    if not last:
      store_probs(jb + 1)
    return carry

  def finish_fixed():
    half = MLA_V_DIM
    for p in range(MLA_HEADS // 2):
      a, b = acc_ref[2 * p], acc_ref[2 * p + 1]
      out_t = jnp.concatenate([a[0:half] / a[half:2 * half], b[half:2 * half] / b[0:half]], axis=0)
      o_ref[0, :, p * LANES:(p + 1) * LANES] = out_t.T.astype(BF16)

  def online_block(jb, carry, masked):
    ms, accs = carry
    new_m, new_acc = [], []
    for h in range(MLA_HEADS):
      s, tile = score(jb, h), value_tile(jb, h)
      if masked:
        s = jnp.where(visible(jb), s, NEG_INF)
      m_new = jnp.maximum(ms[h], jnp.max(s, axis=-1, keepdims=True))
      pr = jnp.exp(s - m_new)
      new_m.append(m_new)
      new_acc.append(jnp.exp(ms[h] - m_new) * accs[h] + _dot(pr.astype(BF16), tile))
    return tuple(new_m), tuple(new_acc)

  def finish(accs):
    for p in range(MLA_HEADS // 2):
      a, b = accs[2 * p], accs[2 * p + 1]
      out = jnp.where(lo, a / pltpu.roll(a, MLA_V_DIM, 1), b / pltpu.roll(b, MLA_V_DIM, 1))
      o_ref[0, :, p * LANES:(p + 1) * LANES] = out.astype(BF16)

  fixed = fixed_ref[0, 0] == 1

  @pl.when(fixed & (i == 0))
  def _():
    for p in range(MLA_HEADS // 2):
      for c0 in range(0, v_ref.shape[1], tk):
        vt_ref[p, :, c0:c0 + tk] = v_ref[0, c0:c0 + tk, p * LANES:(p + 1) * LANES].astype(F32).T.astype(BF16)

  @pl.when(fixed)
  def _():
    acc_ref[...] = jnp.zeros_like(acc_ref)
    store_probs(0)
    lax.fori_loop(0, nfull, functools.partial(fixed_block, last=False), 0)
    fixed_block(nfull, 0, True)
    finish_fixed()

  @pl.when(jnp.logical_not(fixed))
  def _():
    zeros = tuple(jnp.zeros((tq, LANES), F32) for _ in range(MLA_HEADS))
    carry = (tuple(jnp.full((tq, 1), NEG_INF, F32) for _ in range(MLA_HEADS)), zeros)
    carry = lax.fori_loop(0, nfull, functools.partial(online_block, masked=False), carry)
    finish(online_block(nfull, carry, True)[1])


def _mla(fixed_shift, q, k, v, tq, tk):
  b, t, _ = q.shape
  assert tq == tk, "one masked diagonal block per query tile"
  return pl.pallas_call(
      functools.partial(_mla_kernel, tq=tq, tk=tk),
      out_shape=jax.ShapeDtypeStruct((b, t, MLA_WIDTH), BF16),
      grid=(b, t // tq),
      in_specs=[pl.BlockSpec(memory_space=pltpu.SMEM),
                pl.BlockSpec((1, tq, MLA_HEADS * MLA_SLOT), lambda bi, i: (bi, i, 0)),
                pl.BlockSpec((1, t, MLA_HEADS * MLA_SLOT), lambda bi, i: (bi, 0, 0)),
                pl.BlockSpec((1, t, MLA_WIDTH), lambda bi, i: (bi, 0, 0))],
      out_specs=pl.BlockSpec((1, tq, MLA_WIDTH), lambda bi, i: (bi, i, 0)),
      scratch_shapes=[pltpu.VMEM((MLA_HEADS // 2, LANES, t), BF16),
                      pltpu.VMEM((MLA_HEADS, tk, tq), BF16),
                      pltpu.VMEM((MLA_HEADS, LANES, tq), F32)],
      compiler_params=pltpu.CompilerParams(
          dimension_semantics=("arbitrary", "arbitrary"), vmem_limit_bytes=VMEM_LIMIT_BYTES),
      name="mla_attention",
  )(fixed_shift, q, k, v)


SB_STATIC_BLOCKS = 3
SB_CUMSUM_PARTS = 1


def _sb_kernel(q_ref, k_ref, v_ref, o_ref, *, tq, tk):
  for u in range(tq // tk):
    _sb_subtile(q_ref, k_ref, v_ref, o_ref, pl.program_id(1) * (tq // tk) + u, slice(u * tk, (u + 1) * tk), tk)


def _sb_subtile(q_ref, k_ref, v_ref, o_ref, i, rows, tk):
  tq = tk
  pairs = SB_HEADS // 2
  lo = _iota((tq, LANES), 1) < SB_HEAD_DIM
  vlo = _iota((tk, LANES), 1) < SB_HEAD_DIM
  rr = _iota((tk, tk + LANES), 0)
  cc = _iota((tk, tk + LANES), 1)
  tri = jnp.where((rr >= cc) | (cc >= tk), 1.0, 0.0).astype(BF16)

  def sweep(jbs, runs, accs, guard):
    qpos = i * tq + _iota((tq, tk), 0)
    stricts, kbs, vcats = [], [], []
    for jb in jbs:
      ks = pl.multiple_of(jnp.maximum(jb, 0) * tk, tk)
      kpos = jb * tk + _iota((tq, tk), 1)
      strict = kpos < qpos
      stricts.append(strict & (kpos >= 0) if guard else strict)
      kbs.append([k_ref[0, pl.ds(ks, tk), p * LANES:(p + 1) * LANES] for p in range(pairs)])
      vc = []
      for p in range(pairs):
        vb = v_ref[0, pl.ds(ks, tk), p * LANES:(p + 1) * LANES]
        vzero = jnp.zeros_like(vb)
        vc.append(jnp.concatenate([jnp.where(vlo, vb, vzero), jnp.where(vlo, vzero, vb)], axis=0))
      vcats.append(vc)
    qs = []
    for p in range(pairs):
      qp = q_ref[0, rows, p * LANES:(p + 1) * LANES]
      qzero = jnp.zeros_like(qp)
      qs += [jnp.where(lo, qp, qzero), jnp.where(lo, qzero, qp)]
    heads = range(SB_HEADS)
    z = [[_dot(qs[h], kbs[d][h // 2], _NT) for h in heads] for d in range(len(jbs))]
    ls = [[jnp.where(stricts[d], -_softplus(z[d][h]), 0.0) for h in heads] for d in range(len(jbs))]
    cs = [[_mm(ls[d][h], tri, pa=SB_CUMSUM_PARTS, pb=1) for h in heads] for d in range(len(jbs))]
    runs = list(runs)
    weights = [[] for _ in range(pairs)]
    for d in range(len(jbs)):
      for h in heads:
        logw = z[d][h] + cs[d][h][:, 0:tk] + jnp.concatenate([runs[h]] * (tk // LANES), axis=-1)
        weights[h // 2].append(jnp.where(stricts[d], jnp.exp(logw), 0.0).astype(BF16))
        runs[h] = runs[h] + cs[d][h][:, tk:tk + LANES]
    new_accs = []
    for p in range(pairs):
      vall = jnp.concatenate([vcats[d][p] for d in range(len(jbs))], axis=0)
      new_accs.append(accs[p] + _dot(jnp.concatenate(weights[p], axis=1), vall))
    return tuple(runs), tuple(new_accs)

  zr = jnp.zeros((tq, LANES), F32)
  start = (i + 1) * (tq // tk) - 1
  runs, accs = sweep([start - d for d in range(SB_STATIC_BLOCKS)], (zr,) * SB_HEADS, (zr,) * pairs, True)

  def cond(state):
    jb, runs, _ = state
    top = functools.reduce(jnp.maximum, runs)
    return (jb >= 0) & (jnp.max(top) > SB_LOG_FLOOR)

  def body(state):
    jb, runs, accs = state
    runs, accs = sweep([jb], runs, accs, False)
    return jb - 1, runs, accs

  _, _, accs = lax.while_loop(cond, body, (start - SB_STATIC_BLOCKS, runs, accs))
  for p in range(pairs):
    o_ref[0, rows, p * LANES:(p + 1) * LANES] = accs[p].astype(BF16)


def _sb(q, k, v, tq, tk):
  b, t, _ = q.shape
  return pl.pallas_call(
      functools.partial(_sb_kernel, tq=tq, tk=tk),
      out_shape=jax.ShapeDtypeStruct((b, t, SB_WIDTH), BF16),
      grid=(b, t // tq),
      in_specs=[pl.BlockSpec((1, tq, SB_WIDTH), lambda bi, i: (bi, i, 0)),
                pl.BlockSpec((1, t, SB_WIDTH), lambda bi, i: (bi, 0, 0)),
                pl.BlockSpec((1, t, SB_WIDTH), lambda bi, i: (bi, 0, 0))],
      out_specs=pl.BlockSpec((1, tq, SB_WIDTH), lambda bi, i: (bi, i, 0)),
      compiler_params=pltpu.CompilerParams(
          dimension_semantics=("arbitrary", "arbitrary"), vmem_limit_bytes=VMEM_LIMIT_BYTES),
      name="sb_attention",
  )(q, k, v)


FFN_CHUNKS = (1024, 1024, 768)
assert sum(FFN_CHUNKS) == FFN_HIDDEN


def _outffn_kernel(x_ref, yrw_ref, ymla_ref, ysb_ref, wo_ref, ng_ref, wg_ref, wu_ref, wd_ref, o_ref):
  mix = (_dot(yrw_ref[...], wo_ref[0:RW_WIDTH, :])
         + _dot(ymla_ref[...], wo_ref[RW_WIDTH:RW_WIDTH + MLA_WIDTH, :])
         + _dot(ysb_ref[...], wo_ref[RW_WIDTH + MLA_WIDTH:, :]))
  x1 = x_ref[...] + mix
  hn = (x1 * _rms_scale(x1, D_MODEL) * ng_ref[...]).astype(BF16)
  acc = x1
  start = 0
  for width in FFN_CHUNKS:
    cols = slice(start, start + width)
    gate = _dot(hn, wg_ref[:, cols])
    up = _dot(hn, wu_ref[:, cols])
    act = (gate * _sigmoid(gate) * up).astype(BF16)
    acc = acc + _dot(act, wd_ref[cols, :])
    start += width
  o_ref[...] = acc


def _outffn(x, yrw, ymla, ysb, ws, l, tm):
  n = x.shape[0]
  row = lambda i: (i, 0)
  resident = lambda a: _layer_spec(a, l, pipeline_mode=pl.Buffered(1))
  weights = [ws[k] for k in ("wo", "fg", "wg", "wu", "wd")]
  return pl.pallas_call(
      _outffn_kernel,
      out_shape=jax.ShapeDtypeStruct((n, D_MODEL), F32),
      grid=(n // tm,),
      in_specs=[pl.BlockSpec((tm, D_MODEL), row),
                pl.BlockSpec((tm, RW_WIDTH), row),
                pl.BlockSpec((tm, MLA_WIDTH), row),
                pl.BlockSpec((tm, SB_WIDTH), row)] + [resident(w) for w in weights],
      out_specs=pl.BlockSpec((tm, D_MODEL), row),
      compiler_params=pltpu.CompilerParams(
          dimension_semantics=("arbitrary",), vmem_limit_bytes=VMEM_LIMIT_BYTES),
      name="outproj_ffn",
  )(x, yrw, ymla, ysb, *weights)


def _pad_heads(w, heads, dim, slot):
  lead = w.shape[:-1]
  w = w.reshape(lead + (heads, dim))
  w = jnp.pad(w, [(0, 0)] * len(lead) + [(0, 0), (0, slot - dim)])
  return w.reshape(lead + (heads * slot,))


def _stack_weights(attn_norm_g, w_in, rw_shift_mu, rw_w_up, rw_w0, rw_a_up, rw_a0, rw_g_up, rw_k_k,
                   rw_k_a, rw_r_k, rw_ln_g, rw_ln_b, mla_cq_norm_g, mla_ckv_norm_g, mla_w_uq, mla_w_ukv,
                   mla_q_norm_g, mla_k_norm_g, w_o, ffn_norm_g, ffn_w_gate, ffn_w_up, ffn_w_down):
  depth = w_in.shape[0]
  vec = lambda a: a.reshape(depth, 1, -1)
  w16 = w_in.astype(BF16)
  o = RW_COLS + MLA_Q_RANK + MLA_KV_RANK
  rope_slot = jnp.pad(w16[:, :, o:o + MLA_ROPE_DIM],
                      ((0, 0), (0, 0), (MLA_NOPE_DIM, LANES - MLA_NOPE_DIM - MLA_ROPE_DIM)))
  win_p = jnp.concatenate([w16[:, :, :o], w16[:, :, o + MLA_ROPE_DIM:], rope_slot], axis=2)
  ukv = mla_w_ukv.reshape(depth, MLA_KV_RANK, MLA_HEADS, MLA_NOPE_DIM + MLA_V_DIM)
  wuk = _pad_heads(ukv[..., :MLA_NOPE_DIM].reshape(depth, MLA_KV_RANK, -1), MLA_HEADS, MLA_NOPE_DIM, MLA_SLOT)
  wuv = ukv[..., MLA_NOPE_DIM:].reshape(depth, MLA_KV_RANK, MLA_HEADS * MLA_V_DIM)
  head_gain = lambda g: jnp.tile(jnp.pad(g, ((0, 0), (0, MLA_SLOT - MLA_QK_DIM))), (1, MLA_HEADS)).reshape(depth, 1, -1)
  bound = (MLA_QK_DIM ** 0.5) * jnp.max(jnp.abs(mla_q_norm_g), axis=1) * jnp.max(jnp.abs(mla_k_norm_g), axis=1)
  shift = (MLA_BOUND_MARGIN * bound).reshape(depth, 1, 1)
  fixed_shift = shift <= MLA_MAX_SHIFT
  bias_lane = jnp.tile(jnp.arange(MLA_SLOT) == MLA_QK_DIM, MLA_HEADS).reshape(1, 1, -1)
  qb = jnp.where(bias_lane & fixed_shift, -shift, 0.0).astype(F32)
  kb = jnp.broadcast_to(jnp.where(bias_lane, 1.0, 0.0).astype(F32), qb.shape)
  return dict(
      qb=qb, kb=kb, fixed_shift=fixed_shift.astype(jnp.int32),
      ng=vec(attn_norm_g), win=win_p, mu=vec(rw_shift_mu), wup=rw_w_up, w0=vec(rw_w0),
      aup=rw_a_up, a0=vec(rw_a0), gup=rw_g_up, kk=vec(rw_k_k), ka=vec(rw_k_a),
      rk=vec(rw_r_k), lng=vec(rw_ln_g), lnb=vec(rw_ln_b),
      cqg=vec(mla_cq_norm_g), ckvg=vec(mla_ckv_norm_g),
      wuq=_pad_heads(mla_w_uq, MLA_HEADS, MLA_QK_DIM, MLA_SLOT).astype(BF16),
      wuk=wuk.astype(BF16), wuv=wuv.astype(BF16),
      qg=head_gain(mla_q_norm_g), kg=head_gain(mla_k_norm_g),
      wo=w_o.astype(BF16), fg=vec(ffn_norm_g),
      wg=ffn_w_gate.astype(BF16), wu=ffn_w_up.astype(BF16), wd=ffn_w_down.astype(BF16))


def _layer_spec(a, l, **kwargs):
  zeros = (0,) * (a.ndim - 1)
  return pl.BlockSpec((None,) + a.shape[1:], lambda *_: (l,) + zeros, **kwargs)


def kernel(x, positions, attn_norm_g, w_in, rw_shift_mu, rw_w_up, rw_w0, rw_a_up, rw_a0, rw_g_up, rw_k_k, rw_k_a, rw_r_k, rw_ln_g, rw_ln_b, mla_cq_norm_g, mla_ckv_norm_g, mla_w_uq, mla_w_ukv, mla_q_norm_g, mla_k_norm_g, w_o, ffn_norm_g, ffn_w_gate, ffn_w_up, ffn_w_down):
  b, t, d = x.shape
  depth = w_in.shape[0]
  params = (attn_norm_g, w_in, rw_shift_mu, rw_w_up, rw_w0, rw_a_up, rw_a0, rw_g_up, rw_k_k, rw_k_a, rw_r_k,
            rw_ln_g, rw_ln_b, mla_cq_norm_g, mla_ckv_norm_g, mla_w_uq, mla_w_ukv, mla_q_norm_g, mla_k_norm_g,
            w_o, ffn_norm_g, ffn_w_gate, ffn_w_up, ffn_w_down)
  tile = min(TOKEN_TILE, t)
  assert t % tile == 0 and tile % SB_TILE == 0 and tile % CHUNK == 0
  cos, sin = _rope_tables(positions, tile)
  ws = _stack_weights(*params)
  for l in range(depth):
    (r, lwd, k, v, kkn, bvec, g, mq, mk, mv, sq, sk, sv) = _inproj(x, cos, sin, ws, l, tile)
    y_rw = _rwkv(r, lwd, k, v, kkn, bvec, g, ws, l, tc=tile)
    y_mla = _mla(ws["fixed_shift"][l], mq, mk, mv, tq=tile, tk=tile)
    y_sb = _sb(sq, sk, sv, tq=tile, tk=SB_TILE)
    x = _outffn(x.reshape(b * t, d), y_rw.reshape(b * t, -1), y_mla.reshape(b * t, -1),
                y_sb.reshape(b * t, -1), ws, l, tm=tile).reshape(b, t, d)
  return x
```

```python
import functools

import jax
import jax.numpy as jnp
from jax import lax
from jax.experimental import pallas as pl
from jax.experimental.pallas import tpu as pltpu

F32 = jnp.float32
BF16 = jnp.bfloat16

LANES = 128
VMEM_LIMIT_BYTES = 56 * 1024 * 1024

TOKEN_TILE = 512
SB_TILE = LANES

D_MODEL = 1024
CHUNK = 64
CHUNK_SHIFT = 6
RW_HEADS = 8
RW_HEAD_DIM = 64
RW_WIDTH = RW_HEADS * RW_HEAD_DIM
RW_DECAY_LORA = 64
RW_AAA_LORA = 64
RW_GATE_LORA = 128
RW_COLS = 3 * RW_WIDTH + RW_DECAY_LORA + RW_AAA_LORA + RW_GATE_LORA
RW_GN_EPS = 64e-5
MLA_HEADS = 4
MLA_NOPE_DIM = 64
MLA_ROPE_DIM = 32
MLA_V_DIM = 64
MLA_QK_DIM = MLA_NOPE_DIM + MLA_ROPE_DIM
MLA_Q_RANK = 256
MLA_KV_RANK = 128
MLA_WIDTH = MLA_HEADS * MLA_V_DIM
MLA_SLOT = LANES
SB_HEADS = 4
SB_HEAD_DIM = 64
SB_WIDTH = SB_HEADS * SB_HEAD_DIM
FFN_HIDDEN = 2816
ROPE_THETA = 10000.0
NORM_EPS = 1e-6
NEG_INF = -1e30
MLA_BOUND_MARGIN = 1.03
MLA_MAX_SHIFT = 41.0
SB_LOG_FLOOR = -110.0

COL_RW = 0
COL_CQ = RW_COLS
COL_CKV = COL_CQ + MLA_Q_RANK
COL_SBQ = COL_CKV + MLA_KV_RANK
COL_SBK = COL_SBQ + SB_WIDTH
COL_SBV = COL_SBK + SB_WIDTH
COL_ROPE = COL_SBV + SB_WIDTH
IN_COLS_PAD = COL_ROPE + LANES


def _split_bf16(x, parts):
  out = []
  rem = x
  for i in range(parts):
    p = rem.astype(BF16)
    out.append(p)
    if i + 1 < parts:
      rem = rem - p.astype(F32)
  return out


_NN = (((1,), (0,)), ((), ()))
_NT = (((1,), (1,)), ((), ()))
_TN = (((0,), (0,)), ((), ()))


def _dot(a, b, dims=_NN):
  return lax.dot_general(a, b, dims, preferred_element_type=F32)


def _mm(a, b, dims=_NN, pa=2, pb=2):
  as_ = _split_bf16(a, pa) if a.dtype != BF16 else [a]
  bs_ = _split_bf16(b, pb) if b.dtype != BF16 else [b]
  order = max(len(as_), len(bs_))
  acc = None
  for i, ai in enumerate(as_):
    for j, bj in enumerate(bs_):
      if i + j < order:
        t = _dot(ai, bj, dims)
        acc = t if acc is None else acc + t
  return acc


def _sigmoid(x):
  return 1.0 / (1.0 + jnp.exp(-x))


def _softplus(x):
  return jnp.maximum(x, 0.0) + jnp.log(1.0 + jnp.exp(-jnp.abs(x)))


def _iota(shape, dim):
  return lax.broadcasted_iota(jnp.int32, shape, dim)


def _block_ones(block):
  shift = block.bit_length() - 1
  r = _iota((LANES, LANES), 0) >> shift
  c = _iota((LANES, LANES), 1) >> shift
  return jnp.where(r == c, 1.0, 0.0).astype(BF16)


SEG_SUM_PARTS = 1
LORA_PARTS = 1


def _seg_sum(x, bd, fill=None):
  groups = []
  for g in range(0, x.shape[1], LANES):
    groups.append(_mm(x[:, g:g + LANES], bd, pa=SEG_SUM_PARTS, pb=1))
    if fill is not None and g % (2 * LANES) == 0:
      fill()
  return jnp.concatenate(groups, axis=1)


def _rope_kernel(pos_ref, invf_ref, cos_ref, sin_ref):
  ang = pos_ref[0].astype(F32) * invf_ref[...]
  lane = _iota(ang.shape, 1)
  first_half = lane < MLA_NOPE_DIM + MLA_ROPE_DIM // 2
  cos_ref[0] = jnp.cos(ang)
  sin_ref[0] = jnp.where(first_half, -jnp.sin(ang), jnp.sin(ang))


def _rope_tables(positions, tm):
  b, t = positions.shape
  inv_freq = ROPE_THETA ** (-jnp.arange(0, MLA_ROPE_DIM, 2, dtype=F32) / MLA_ROPE_DIM)
  invf = jnp.zeros((1, MLA_SLOT), F32)
  invf = invf.at[0, MLA_NOPE_DIM:MLA_NOPE_DIM + MLA_ROPE_DIM].set(jnp.concatenate([inv_freq, inv_freq]))
  out = jax.ShapeDtypeStruct((b, t, MLA_SLOT), F32)
  return pl.pallas_call(
      _rope_kernel,
      out_shape=(out, out),
      grid=(b, t // tm),
      in_specs=[pl.BlockSpec((1, tm, 1), lambda i, j: (i, j, 0)),
                pl.BlockSpec((1, MLA_SLOT), lambda i, j: (0, 0))],
      out_specs=(pl.BlockSpec((1, tm, MLA_SLOT), lambda i, j: (i, j, 0)),
                 pl.BlockSpec((1, tm, MLA_SLOT), lambda i, j: (i, j, 0))),
      name="rope_tables",
  )(positions.reshape(b, t, 1), invf)


INPROJ_SUBTILE = 256
INPROJ_SECTION = 256


def _rms_scale(x, width):
  ms = jnp.sum(x * x, axis=-1, keepdims=True) * (1.0 / width)
  return lax.rsqrt(ms + NORM_EPS)


def _rotary(x, cos, sin):
  half = MLA_ROPE_DIM // 2
  outs = []
  for h in range(MLA_HEADS):
    xh = x[:, h * MLA_SLOT:(h + 1) * MLA_SLOT]
    lane = _iota(xh.shape, 1)
    up = pltpu.roll(xh, half, 1)
    down = pltpu.roll(xh, MLA_SLOT - half, 1)
    partner = jnp.where(lane >= MLA_NOPE_DIM + half, up, down)
    outs.append(xh * cos + partner * sin)
  return jnp.concatenate(outs, axis=-1)


def _head_rms(x, gain, bd, fill):
  ms = _seg_sum(x * x, bd, fill) * (1.0 / MLA_QK_DIM)
  return x * lax.rsqrt(ms + NORM_EPS) * gain


def _inproj_kernel(x_ref, cos_ref, sin_ref, ng_ref, win_ref, mu_ref, wup_ref, w0_ref, aup_ref, a0_ref,
                   gup_ref, kk_ref, ka_ref, cqg_ref, ckvg_ref, wuq_ref, wuk_ref, wuv_ref, qg_ref, kg_ref,
                   qb_ref, kb_ref,
                   r_out, lw_out, k_out, v_out, kkn_out, b_out, g_out,
                   mq_out, mk_out, mv_out, sq_out, sk_out, sv_out,
                   carry_ref, ha_ref, hb_ref, *, tiles_per_row):
  j = pl.program_id(0)

  @pl.when(j == 0)
  def _():
    hb_ref[...] = jnp.zeros_like(hb_ref)
    carry_ref[...] = jnp.zeros_like(carry_ref)

  tm = x_ref.shape[0]
  sub = min(INPROJ_SUBTILE, tm)
  restart = (lax.rem(j + tiles_per_row - 1, tiles_per_row) == 0) | (j == 0)
  out_refs = (r_out, lw_out, k_out, v_out, kkn_out, b_out, g_out, mq_out, mk_out, mv_out, sq_out, sk_out, sv_out)

  def step(h_new, h_old):
    last_row = jnp.where(restart, 0.0, carry_ref[...])
    for r0 in range(0, tm, sub):
      rows = slice(r0, r0 + sub)
      x = x_ref[rows, :]
      xn = (x * _rms_scale(x, D_MODEL) * ng_ref[...]).astype(BF16)

      def sections(xn=xn, rows=rows):
        for c0 in range(0, IN_COLS_PAD, INPROJ_SECTION):
          h_new[rows, c0:c0 + INPROJ_SECTION] = _dot(xn, win_ref[:, c0:c0 + INPROJ_SECTION])
          yield
      fill = sections()
      last_row = _inproj_rows(h_old.at[rows, :], cos_ref[rows, :], sin_ref[rows, :], last_row,
                              mu_ref, wup_ref, w0_ref, aup_ref, a0_ref, gup_ref, kk_ref, ka_ref,
                              cqg_ref, ckvg_ref, wuq_ref, wuk_ref, wuv_ref, qg_ref, kg_ref, qb_ref, kb_ref,
                              tuple(o.at[rows, :] for o in out_refs), lambda fill=fill: next(fill, None))
      for _ in fill:
        pass
    carry_ref[...] = last_row

  @pl.when(lax.rem(j, 2) == 0)
  def _():
    step(ha_ref, hb_ref)

  @pl.when(lax.rem(j, 2) == 1)
  def _():
    step(hb_ref, ha_ref)


def _inproj_rows(h_ref, cos, sin, prev_row, mu_ref, wup_ref, w0_ref, aup_ref, a0_ref, gup_ref,
                 kk_ref, ka_ref, cqg_ref, ckvg_ref, wuq_ref, wuk_ref, wuv_ref, qg_ref, kg_ref, qb_ref, kb_ref, outs,
                 fill):
  (r_out, lw_out, k_out, v_out, kkn_out, b_out, g_out, mq_out, mk_out, mv_out, sq_out, sk_out, sv_out) = outs
  n = h_ref.shape[0]

  fill()
  h_rw = h_ref[:, COL_RW:COL_RW + RW_COLS]
  row = _iota(h_rw.shape, 0)
  prev = jnp.where(row == 0, prev_row, pltpu.roll(h_rw, 1, 0))
  sh = h_rw + mu_ref[...] * (prev - h_rw)
  r = sh[:, 0:RW_WIDTH]
  k = sh[:, RW_WIDTH:2 * RW_WIDTH]
  v = sh[:, 2 * RW_WIDTH:3 * RW_WIDTH]
  o = 3 * RW_WIDTH
  wd = sh[:, o:o + RW_DECAY_LORA]
  ad = sh[:, o + RW_DECAY_LORA:o + RW_DECAY_LORA + RW_AAA_LORA]
  gd = sh[:, o + RW_DECAY_LORA + RW_AAA_LORA:RW_COLS]

  lora = functools.partial(_mm, pa=LORA_PARTS, pb=LORA_PARTS)
  w_raw = -_softplus(-(w0_ref[...] + lora(jnp.tanh(wd), wup_ref[...]))) - 0.5
  lw_out[...] = -jnp.exp(w_raw)
  fill()
  a = _sigmoid(a0_ref[...] + lora(ad, aup_ref[...]))
  g_out[...] = lora(_sigmoid(gd), gup_ref[...])
  kk = k * kk_ref[...]
  bd64 = _block_ones(RW_HEAD_DIM)
  fill()
  kkn = kk * lax.rsqrt(_seg_sum(kk * kk, bd64, fill) + 1e-12)
  r_out[...] = r
  k_out[...] = k * (1.0 + (a - 1.0) * ka_ref[...])
  v_out[...] = v
  kkn_out[...] = kkn
  b_out[...] = kkn * a

  c_q = h_ref[:, COL_CQ:COL_CQ + MLA_Q_RANK]
  c_kv = h_ref[:, COL_CKV:COL_CKV + MLA_KV_RANK]
  k_rope = h_ref[:, COL_ROPE:COL_ROPE + LANES]
  cqn = (c_q * _rms_scale(c_q, MLA_Q_RANK) * cqg_ref[...]).astype(BF16)
  ckvn = (c_kv * _rms_scale(c_kv, MLA_KV_RANK) * ckvg_ref[...]).astype(BF16)
  bd128 = _block_ones(MLA_SLOT)
  fill()
  q = _head_rms(_dot(cqn, wuq_ref[...]), qg_ref[...], bd128, fill)
  fill()
  kf = _dot(ckvn, wuk_ref[...]) + jnp.concatenate([k_rope] * MLA_HEADS, axis=-1)
  kf = _head_rms(kf, kg_ref[...], bd128, fill)
  mq_out[...] = (_rotary(q, cos, sin) * (MLA_QK_DIM ** -0.5) + qb_ref[...]).astype(BF16)
  fill()
  mk_out[...] = (_rotary(kf, cos, sin) + kb_ref[...]).astype(BF16)
  mv_out[...] = _dot(ckvn, wuv_ref[...]).astype(BF16)

  sq_out[...] = (h_ref[:, COL_SBQ:COL_SBQ + SB_WIDTH] * (SB_HEAD_DIM ** -0.5)).astype(BF16)
  sk_out[...] = h_ref[:, COL_SBK:COL_SBK + SB_WIDTH].astype(BF16)
  sv_out[...] = h_ref[:, COL_SBV:COL_SBV + SB_WIDTH].astype(BF16)
  return h_rw[n - 1:n, :]


def _inproj(x, cos, sin, ws, l, tm):
  b, t, _ = x.shape
  n_tiles = b * t // tm
  cur = lambda j: (jnp.minimum(j, n_tiles - 1), 0)
  done = lambda j: (jnp.maximum(j - 1, 0), 0)
  full = lambda a: _layer_spec(a, l)
  weights = [ws[n] for n in ("ng", "win", "mu", "wup", "w0", "aup", "a0", "gup", "kk", "ka",
                             "cqg", "ckvg", "wuq", "wuk", "wuv", "qg", "kg", "qb", "kb")]
  rw = jax.ShapeDtypeStruct((b * t, RW_WIDTH), F32)
  att4 = jax.ShapeDtypeStruct((b * t, MLA_HEADS * MLA_SLOT), BF16)
  att2 = jax.ShapeDtypeStruct((b * t, SB_WIDTH), BF16)
  out_shape = (rw,) * 7 + (att4, att4, att2, att2, att2, att2)
  out_specs = tuple(pl.BlockSpec((tm, s.shape[-1]), done) for s in out_shape)
  outs = pl.pallas_call(
      functools.partial(_inproj_kernel, tiles_per_row=t // tm),
      out_shape=out_shape,
      grid=(n_tiles + 1,),
      in_specs=[pl.BlockSpec((tm, D_MODEL), cur),
                pl.BlockSpec((tm, MLA_SLOT), done),
                pl.BlockSpec((tm, MLA_SLOT), done)] + [full(w) for w in weights],
      out_specs=out_specs,
      scratch_shapes=[pltpu.VMEM((1, RW_COLS), F32),
                      pltpu.VMEM((tm, IN_COLS_PAD), F32),
                      pltpu.VMEM((tm, IN_COLS_PAD), F32)],
      compiler_params=pltpu.CompilerParams(
          dimension_semantics=("arbitrary",), vmem_limit_bytes=VMEM_LIMIT_BYTES),
      name="inproj",
  )(x.reshape(b * t, -1), cos.reshape(b * t, -1), sin.reshape(b * t, -1), *weights)
  return tuple(o.reshape(b, t, -1) for o in outs)


RW_PASSES_SCORE = 1
RW_PASSES_INV = 1
RW_PASSES_MID = 1
RW_PASSES_STATE = 1


def _rwkv_phase1(items, p_inv, p_score, p_mid):
  c = CHUNK
  n_items = len(items)
  lane = _iota((c, LANES), 1)
  lo = lane < RW_HEAD_DIM
  stack = lambda m: jnp.concatenate([jnp.where(lo, m, 0.0), jnp.where(lo, 0.0, m)], axis=0)
  r2 = _iota((2 * c, 2 * c), 0)
  c2 = _iota((2 * c, 2 * c), 1)
  eye = jnp.where(r2 == c2, 1.0, 0.0).astype(F32)
  r4 = _iota((4 * c, 4 * c), 0)
  c4 = _iota((4 * c, 4 * c), 1)
  keep = ((r4 & (c - 1)) + jnp.where(r4 >= 2 * c, 1, 0)) > (c4 & (c - 1))
  mmm = functools.partial(_mm, pa=p_mid, pb=p_mid)

  def both(key_a, key_b, out_key, parts, rows=None):
    for s in st:
      a = s[key_a] if rows is None else rows(s[key_a])
      s[out_key] = _mm(a, s[key_b], pa=parts, pb=parts)

  st = []
  for at, rt, bt, kt, v, p_end in items:
    a_st, r_st, b_st, k_st, v_st = stack(at), stack(rt), stack(bt), stack(kt), stack(v)
    rhs = jnp.concatenate([b_st, k_st], axis=0)
    sc = _mm(jnp.concatenate([a_st, r_st], axis=0), rhs, _NT, pa=p_score, pb=p_score)
    sc = jnp.where(keep, sc, 0.0)
    st.append(dict(a_st=a_st, r_st=r_st, v_st=v_st, rhs=rhs, p_end=p_end,
                   n=sc[0:2 * c, 0:2 * c], a_ak=sc[0:2 * c, 2 * c:4 * c], r_bk=sc[2 * c:4 * c, :]))
  for s in st:
    s["nd"] = jnp.where((r2 >> 3) == (c2 >> 3), s["n"], 0.0)
  both("nd", "nd", "n2", p_inv)
  both("n2", "n2", "n4", p_inv)
  both("nd", "n2", "n3", p_inv)
  for s in st:
    s["p1"] = eye + s["nd"] + s["n2"] + s["n3"]
  both("p1", "n4", "t", p_inv)
  for s in st:
    s["inv"] = s["p1"] + s["t"]
  shift = 3
  while (1 << shift) < c:
    bs = 1 << shift
    groups = range(2 * c // (2 * bs))
    lower = lambda m, bs=bs, groups=groups: jnp.concatenate(
        [m[g * 2 * bs + bs:(g + 1) * 2 * bs] for g in groups], axis=0)
    sel = ((r2 >> (shift + 1)) == (c2 >> (shift + 1))) & ((r2 >> shift) != (c2 >> shift))
    for s in st:
      s["off"] = jnp.where(sel, s["n"], 0.0)
    both("inv", "off", "t", p_inv, rows=lower)
    both("t", "inv", "t", p_inv)
    for s in st:
      inv, t = s["inv"], s["t"]
      pieces = []
      for g in groups:
        pieces += [inv[g * 2 * bs:g * 2 * bs + bs], inv[g * 2 * bs + bs:(g + 1) * 2 * bs] + t[g * bs:(g + 1) * bs]]
      s["inv"] = jnp.concatenate(pieces, axis=0)
    shift += 1
  both("a_ak", "v_st", "av", p_mid)
  for s in st:
    s["wu"] = mmm(s["inv"], jnp.concatenate([s["a_st"], s["av"]], axis=1))
  out = []
  for s in st:
    wt, ut = s["wu"][:, 0:LANES], s["wu"][:, LANES:2 * LANES]
    r_b, r_k = s["r_bk"][:, 0:2 * c], s["r_bk"][:, 2 * c:4 * c]
    rwy = mmm(r_b, s["wu"])
    y0 = rwy[:, LANES:2 * LANES] + mmm(r_k, s["v_st"])
    m = (eye + mmm(wt, s["rhs"][0:2 * c], _TN)) * s["p_end"]
    s1 = mmm(jnp.concatenate([ut, s["v_st"]], axis=0), s["rhs"], _TN) * s["p_end"]
    out.append((s["r_st"] + rwy[:, 0:LANES], y0, m, s1))
  assert len(out) == n_items
  return out


def _rwkv_kernel(r_ref, lw_ref, k_ref, v_ref, kkn_ref, b_ref, g_ref, rk_ref, lng_ref, lnb_ref,
                 o_ref, s_ref, y_ref):
  j = pl.program_id(1)
  pairs = RW_WIDTH // LANES

  @pl.when(j == 0)
  def _():
    s_ref[...] = jnp.zeros_like(s_ref)

  tc = r_ref.shape[1]
  rr = _iota((CHUNK, CHUNK), 0)
  cc = _iota((CHUNK, CHUNK), 1)
  tri = jnp.where(rr >= cc, 1.0, 0.0).astype(BF16)
  items = []
  for ci in range(tc // CHUNK):
    rows = slice(ci * CHUNK, (ci + 1) * CHUNK)
    lw = lw_ref[0, rows, :]
    lp = _mm(tri, lw, pa=1, pb=3)
    e_in = jnp.exp(lp)
    e_ex = jnp.exp(lp - lw)
    e_neg = jnp.exp(-lp)
    at = -kkn_ref[0, rows, :] * e_ex
    rt = r_ref[0, rows, :] * e_in
    bt = b_ref[0, rows, :] * e_neg
    kt = k_ref[0, rows, :] * e_neg
    v = v_ref[0, rows, :]
    p_end = e_in[CHUNK - 1:CHUNK, :]
    for p in range(pairs):
      ln = slice(p * LANES, (p + 1) * LANES)
      items.append((at[:, ln], rt[:, ln], bt[:, ln], kt[:, ln], v[:, ln], p_end[:, ln]))
  pre = _rwkv_phase1(items, RW_PASSES_INV, RW_PASSES_SCORE, RW_PASSES_MID)

  for ci in range(tc // CHUNK):
    rows = slice(ci * CHUNK, (ci + 1) * CHUNK)
    for p in range(pairs):
      rw, y0, m, s1 = pre[ci * pairs + p]
      s = s_ref[p]
      y_st = _mm(rw, s, _NT, pa=RW_PASSES_STATE, pb=RW_PASSES_STATE) + y0
      y_ref[rows, p * LANES:(p + 1) * LANES] = y_st[0:CHUNK] + y_st[CHUNK:2 * CHUNK]
      s_ref[p] = _mm(s, m, pa=RW_PASSES_STATE, pb=RW_PASSES_STATE) + s1

  y = y_ref[...]
  bd = _block_ones(RW_HEAD_DIM)
  inv_d = 1.0 / RW_HEAD_DIM
  mean = _seg_sum(y, bd) * inv_d
  d = y - mean
  var = _seg_sum(d * d, bd) * inv_d
  yn = d * lax.rsqrt(var + RW_GN_EPS) * lng_ref[...] + lnb_ref[...]
  bonus = _seg_sum(r_ref[0] * k_ref[0] * rk_ref[...], bd) * v_ref[0]
  o_ref[0] = ((yn + bonus) * g_ref[0]).astype(BF16)


def _rwkv(r, lw, k, v, kkn, bvec, g, ws, l, tc):
  b, t, _ = r.shape
  row = lambda i, j: (i, j, 0)
  seq = pl.BlockSpec((1, tc, RW_WIDTH), row)
  rk, lng, lnb = ws["rk"], ws["lng"], ws["lnb"]
  return pl.pallas_call(
      _rwkv_kernel,
      out_shape=jax.ShapeDtypeStruct((b, t, RW_WIDTH), BF16),
      grid=(b, t // tc),
      in_specs=[seq] * 7 + [_layer_spec(a, l) for a in (rk, lng, lnb)],
      out_specs=seq,
      scratch_shapes=[pltpu.VMEM((RW_WIDTH // LANES, LANES, LANES), F32),
                      pltpu.VMEM((tc, RW_WIDTH), F32)],
      compiler_params=pltpu.CompilerParams(
          dimension_semantics=("arbitrary", "arbitrary"), vmem_limit_bytes=VMEM_LIMIT_BYTES),
      name="rwkv",
  )(r, lw, k, v, kkn, bvec, g, rk, lng, lnb)


def _mla_kernel(fixed_ref, q_ref, k_ref, v_ref, o_ref, vt_ref, p_ref, acc_ref, *, tq, tk):
  i = pl.program_id(1)
  lo = _iota((tq, LANES), 1) < MLA_V_DIM
  vlo = _iota((tk, LANES), 1) < MLA_V_DIM
  nfull = i * (tq // tk)

  def score(jb, h):
    ks = pl.multiple_of(jb * tk, tk)
    q = q_ref[0, :, h * MLA_SLOT:(h + 1) * MLA_SLOT]
    return _dot(q, k_ref[0, pl.ds(ks, tk), h * MLA_SLOT:(h + 1) * MLA_SLOT], _NT)

  def value_tile(jb, h):
    ks = pl.multiple_of(jb * tk, tk)
    vb = v_ref[0, pl.ds(ks, tk), (h // 2) * LANES:(h // 2 + 1) * LANES]
    one = jnp.ones_like(vb)
    return jnp.where(vlo, vb, one) if h % 2 == 0 else jnp.where(vlo, one, vb)

  def visible(jb):
    qc = (i * tq + _iota((tq, tk), 0)) >> CHUNK_SHIFT
    kc = (jb * tk + _iota((tq, tk), 1)) >> CHUNK_SHIFT
    return kc <= qc

  def store_probs(jb):
    ks = pl.multiple_of(jb * tk, tk)
    for h in range(MLA_HEADS):
      cols = slice(h * MLA_SLOT, (h + 1) * MLA_SLOT)
      p_ref[h] = jnp.exp(_dot(k_ref[0, pl.ds(ks, tk), cols], q_ref[0, :, cols], _NT)).astype(BF16)

  def value_rows(jb, h):
    ks = pl.multiple_of(jb * tk, tk)
    vt = vt_ref[h // 2, :, pl.ds(ks, tk)]
    one = jnp.ones((MLA_V_DIM, tk), BF16)
    return (jnp.concatenate([vt[0:MLA_V_DIM], one], axis=0) if h % 2 == 0
            else jnp.concatenate([one, vt[MLA_V_DIM:2 * MLA_V_DIM]], axis=0))

  def fixed_block(jb, carry, last):
    for h in range(MLA_HEADS):
      pr = p_ref[h]
      if last:
        kc = (jb * tk + _iota((tk, tq), 0)) >> CHUNK_SHIFT
        qc = (i * tq + _iota((tk, tq), 1)) >> CHUNK_SHIFT
        pr = jnp.where(kc <= qc, pr.astype(F32), 0.0).astype(BF16)
      acc_ref[h] += _dot(value_rows(jb, h), pr)
    if not last:
      store_probs(jb + 1)
    return carry

  def finish_fixed():
    half = MLA_V_DIM
    for p in range(MLA_HEADS // 2):
      a, b = acc_ref[2 * p], acc_ref[2 * p + 1]
      out_t = jnp.concatenate([a[0:half] / a[half:2 * half], b[half:2 * half] / b[0:half]], axis=0)
      o_ref[0, :, p * LANES:(p + 1) * LANES] = out_t.T.astype(BF16)

  def online_block(jb, carry, masked):
    ms, accs = carry
    new_m, new_acc = [], []
    for h in range(MLA_HEADS):
      s, tile = score(jb, h), value_tile(jb, h)
      if masked:
        s = jnp.where(visible(jb), s, NEG_INF)
      m_new = jnp.maximum(ms[h], jnp.max(s, axis=-1, keepdims=True))
      pr = jnp.exp(s - m_new)
      new_m.append(m_new)
      new_acc.append(jnp.exp(ms[h] - m_new) * accs[h] + _dot(pr.astype(BF16), tile))
    return tuple(new_m), tuple(new_acc)

  def finish(accs):
    for p in range(MLA_HEADS // 2):
      a, b = accs[2 * p], accs[2 * p + 1]
      out = jnp.where(lo, a / pltpu.roll(a, MLA_V_DIM, 1), b / pltpu.roll(b, MLA_V_DIM, 1))
      o_ref[0, :, p * LANES:(p + 1) * LANES] = out.astype(BF16)

  fixed = fixed_ref[0, 0] == 1

  @pl.when(fixed & (i == 0))
  def _():
    for p in range(MLA_HEADS // 2):
      for c0 in range(0, v_ref.shape[1], tk):
        vt_ref[p, :, c0:c0 + tk] = v_ref[0, c0:c0 + tk, p * LANES:(p + 1) * LANES].astype(F32).T.astype(BF16)

  @pl.when(fixed)
  def _():
    acc_ref[...] = jnp.zeros_like(acc_ref)
    store_probs(0)
    lax.fori_loop(0, nfull, functools.partial(fixed_block, last=False), 0)
    fixed_block(nfull, 0, True)
    finish_fixed()

  @pl.when(jnp.logical_not(fixed))
  def _():
    zeros = tuple(jnp.zeros((tq, LANES), F32) for _ in range(MLA_HEADS))
    carry = (tuple(jnp.full((tq, 1), NEG_INF, F32) for _ in range(MLA_HEADS)), zeros)
    carry = lax.fori_loop(0, nfull, functools.partial(online_block, masked=False), carry)
    finish(online_block(nfull, carry, True)[1])


def _mla(fixed_shift, q, k, v, tq, tk):
  b, t, _ = q.shape
  assert tq == tk, "one masked diagonal block per query tile"
  return pl.pallas_call(
      functools.partial(_mla_kernel, tq=tq, tk=tk),
      out_shape=jax.ShapeDtypeStruct((b, t, MLA_WIDTH), BF16),
      grid=(b, t // tq),
      in_specs=[pl.BlockSpec(memory_space=pltpu.SMEM),
                pl.BlockSpec((1, tq, MLA_HEADS * MLA_SLOT), lambda bi, i: (bi, i, 0)),
                pl.BlockSpec((1, t, MLA_HEADS * MLA_SLOT), lambda bi, i: (bi, 0, 0)),
                pl.BlockSpec((1, t, MLA_WIDTH), lambda bi, i: (bi, 0, 0))],
      out_specs=pl.BlockSpec((1, tq, MLA_WIDTH), lambda bi, i: (bi, i, 0)),
      scratch_shapes=[pltpu.VMEM((MLA_HEADS // 2, LANES, t), BF16),
                      pltpu.VMEM((MLA_HEADS, tk, tq), BF16),
                      pltpu.VMEM((MLA_HEADS, LANES, tq), F32)],
      compiler_params=pltpu.CompilerParams(
          dimension_semantics=("arbitrary", "arbitrary"), vmem_limit_bytes=VMEM_LIMIT_BYTES),
      name="mla_attention",
  )(fixed_shift, q, k, v)


SB_STATIC_BLOCKS = 3
SB_CUMSUM_PARTS = 1


def _sb_kernel(q_ref, k_ref, v_ref, o_ref, *, tq, tk):
  tails = [_sb_subtile(q_ref, k_ref, v_ref, o_ref, pl.program_id(1) * (tq // tk) + u,
                       slice(u * tk, (u + 1) * tk), tk) for u in range(tq // tk)]
  for tail in tails:
    tail()


def _sb_subtile(q_ref, k_ref, v_ref, o_ref, i, rows, tk):
  tq = tk
  pairs = SB_HEADS // 2
  lo = _iota((tq, LANES), 1) < SB_HEAD_DIM
  vlo = _iota((tk, LANES), 1) < SB_HEAD_DIM
  rr = _iota((tk, tk + LANES), 0)
  cc = _iota((tk, tk + LANES), 1)
  tri = jnp.where((rr >= cc) | (cc >= tk), 1.0, 0.0).astype(BF16)

  def sweep(jbs, runs, accs, guard):
    qpos = i * tq + _iota((tq, tk), 0)
    stricts, kbs, vcats = [], [], []
    for jb in jbs:
      ks = pl.multiple_of(jnp.maximum(jb, 0) * tk, tk)
      kpos = jb * tk + _iota((tq, tk), 1)
      strict = kpos < qpos
      stricts.append(strict & (kpos >= 0) if guard else strict)
      kbs.append([k_ref[0, pl.ds(ks, tk), p * LANES:(p + 1) * LANES] for p in range(pairs)])
      vc = []
      for p in range(pairs):
        vb = v_ref[0, pl.ds(ks, tk), p * LANES:(p + 1) * LANES]
        vzero = jnp.zeros_like(vb)
        vc.append(jnp.concatenate([jnp.where(vlo, vb, vzero), jnp.where(vlo, vzero, vb)], axis=0))
      vcats.append(vc)
    qs = []
    for p in range(pairs):
      qp = q_ref[0, rows, p * LANES:(p + 1) * LANES]
      qzero = jnp.zeros_like(qp)
      qs += [jnp.where(lo, qp, qzero), jnp.where(lo, qzero, qp)]
    heads = range(SB_HEADS)
    z = [[_dot(qs[h], kbs[d][h // 2], _NT) for h in heads] for d in range(len(jbs))]
    ls = [[jnp.where(stricts[d], -_softplus(z[d][h]), 0.0) for h in heads] for d in range(len(jbs))]
    cs = [[_mm(ls[d][h], tri, pa=SB_CUMSUM_PARTS, pb=1) for h in heads] for d in range(len(jbs))]
    runs = list(runs)
    weights = [[] for _ in range(pairs)]
    for d in range(len(jbs)):
      for h in heads:
        logw = z[d][h] + cs[d][h][:, 0:tk] + jnp.concatenate([runs[h]] * (tk // LANES), axis=-1)
        weights[h // 2].append(jnp.where(stricts[d], jnp.exp(logw), 0.0).astype(BF16))
        runs[h] = runs[h] + cs[d][h][:, tk:tk + LANES]
    new_accs = []
    for p in range(pairs):
      vall = jnp.concatenate([vcats[d][p] for d in range(len(jbs))], axis=0)
      new_accs.append(accs[p] + _dot(jnp.concatenate(weights[p], axis=1), vall))
    return tuple(runs), tuple(new_accs)

  zr = jnp.zeros((tq, LANES), F32)
  start = (i + 1) * (tq // tk) - 1
  runs, accs = sweep([start - d for d in range(SB_STATIC_BLOCKS)], (zr,) * SB_HEADS, (zr,) * pairs, True)

  def cond(state):
    jb, runs, _ = state
    top = functools.reduce(jnp.maximum, runs)
    return (jb >= 0) & (jnp.max(top) > SB_LOG_FLOOR)

  def body(state):
    jb, runs, accs = state
    runs, accs = sweep([jb], runs, accs, False)
    return jb - 1, runs, accs

  def tail():
    _, _, out = lax.while_loop(cond, body, (start - SB_STATIC_BLOCKS, runs, accs))
    for p in range(pairs):
      o_ref[0, rows, p * LANES:(p + 1) * LANES] = out[p].astype(BF16)

  return tail


def _sb(q, k, v, tq, tk):
  b, t, _ = q.shape
  return pl.pallas_call(
      functools.partial(_sb_kernel, tq=tq, tk=tk),
      out_shape=jax.ShapeDtypeStruct((b, t, SB_WIDTH), BF16),
      grid=(b, t // tq),
      in_specs=[pl.BlockSpec((1, tq, SB_WIDTH), lambda bi, i: (bi, i, 0)),
                pl.BlockSpec((1, t, SB_WIDTH), lambda bi, i: (bi, 0, 0)),
                pl.BlockSpec((1, t, SB_WIDTH), lambda bi, i: (bi, 0, 0))],
      out_specs=pl.BlockSpec((1, tq, SB_WIDTH), lambda bi, i: (bi, i, 0)),
      compiler_params=pltpu.CompilerParams(
          dimension_semantics=("arbitrary", "arbitrary"), vmem_limit_bytes=VMEM_LIMIT_BYTES),
      name="sb_attention",
  )(q, k, v)


FFN_CHUNKS = (1024, 1024, 768)
assert sum(FFN_CHUNKS) == FFN_HIDDEN


def _outffn_kernel(x_ref, yrw_ref, ymla_ref, ysb_ref, wo_ref, ng_ref, wg_ref, wu_ref, wd_ref, o_ref):
  mix = (_dot(yrw_ref[...], wo_ref[0:RW_WIDTH, :])
         + _dot(ymla_ref[...], wo_ref[RW_WIDTH:RW_WIDTH + MLA_WIDTH, :])
         + _dot(ysb_ref[...], wo_ref[RW_WIDTH + MLA_WIDTH:, :]))
  x1 = x_ref[...] + mix
  hn = (x1 * _rms_scale(x1, D_MODEL) * ng_ref[...]).astype(BF16)
  acc = x1
  start = 0
  for width in FFN_CHUNKS:
    cols = slice(start, start + width)
    gate = _dot(hn, wg_ref[:, cols])
    up = _dot(hn, wu_ref[:, cols])
    act = (gate * _sigmoid(gate) * up).astype(BF16)
    acc = acc + _dot(act, wd_ref[cols, :])
    start += width
  o_ref[...] = acc


def _outffn(x, yrw, ymla, ysb, ws, l, tm):
  n = x.shape[0]
  row = lambda i: (i, 0)
  resident = lambda a: _layer_spec(a, l, pipeline_mode=pl.Buffered(1))
  weights = [ws[k] for k in ("wo", "fg", "wg", "wu", "wd")]
  return pl.pallas_call(
      _outffn_kernel,
      out_shape=jax.ShapeDtypeStruct((n, D_MODEL), F32),
      grid=(n // tm,),
      in_specs=[pl.BlockSpec((tm, D_MODEL), row),
                pl.BlockSpec((tm, RW_WIDTH), row),
                pl.BlockSpec((tm, MLA_WIDTH), row),
                pl.BlockSpec((tm, SB_WIDTH), row)] + [resident(w) for w in weights],
      out_specs=pl.BlockSpec((tm, D_MODEL), row),
      compiler_params=pltpu.CompilerParams(
          dimension_semantics=("arbitrary",), vmem_limit_bytes=VMEM_LIMIT_BYTES),
      name="outproj_ffn",
  )(x, yrw, ymla, ysb, *weights)


def _pad_heads(w, heads, dim, slot):
  lead = w.shape[:-1]
  w = w.reshape(lead + (heads, dim))
  w = jnp.pad(w, [(0, 0)] * len(lead) + [(0, 0), (0, slot - dim)])
  return w.reshape(lead + (heads * slot,))


def _stack_weights(attn_norm_g, w_in, rw_shift_mu, rw_w_up, rw_w0, rw_a_up, rw_a0, rw_g_up, rw_k_k,
                   rw_k_a, rw_r_k, rw_ln_g, rw_ln_b, mla_cq_norm_g, mla_ckv_norm_g, mla_w_uq, mla_w_ukv,
                   mla_q_norm_g, mla_k_norm_g, w_o, ffn_norm_g, ffn_w_gate, ffn_w_up, ffn_w_down):
  depth = w_in.shape[0]
  vec = lambda a: a.reshape(depth, 1, -1)
  w16 = w_in.astype(BF16)
  o = RW_COLS + MLA_Q_RANK + MLA_KV_RANK
  rope_slot = jnp.pad(w16[:, :, o:o + MLA_ROPE_DIM],
                      ((0, 0), (0, 0), (MLA_NOPE_DIM, LANES - MLA_NOPE_DIM - MLA_ROPE_DIM)))
  win_p = jnp.concatenate([w16[:, :, :o], w16[:, :, o + MLA_ROPE_DIM:], rope_slot], axis=2)
  ukv = mla_w_ukv.reshape(depth, MLA_KV_RANK, MLA_HEADS, MLA_NOPE_DIM + MLA_V_DIM)
  wuk = _pad_heads(ukv[..., :MLA_NOPE_DIM].reshape(depth, MLA_KV_RANK, -1), MLA_HEADS, MLA_NOPE_DIM, MLA_SLOT)
  wuv = ukv[..., MLA_NOPE_DIM:].reshape(depth, MLA_KV_RANK, MLA_HEADS * MLA_V_DIM)
  head_gain = lambda g: jnp.tile(jnp.pad(g, ((0, 0), (0, MLA_SLOT - MLA_QK_DIM))), (1, MLA_HEADS)).reshape(depth, 1, -1)
  bound = (MLA_QK_DIM ** 0.5) * jnp.max(jnp.abs(mla_q_norm_g), axis=1) * jnp.max(jnp.abs(mla_k_norm_g), axis=1)
  shift = (MLA_BOUND_MARGIN * bound).reshape(depth, 1, 1)
  fixed_shift = shift <= MLA_MAX_SHIFT
  bias_lane = jnp.tile(jnp.arange(MLA_SLOT) == MLA_QK_DIM, MLA_HEADS).reshape(1, 1, -1)
  qb = jnp.where(bias_lane & fixed_shift, -shift, 0.0).astype(F32)
  kb = jnp.broadcast_to(jnp.where(bias_lane, 1.0, 0.0).astype(F32), qb.shape)
  return dict(
      qb=qb, kb=kb, fixed_shift=fixed_shift.astype(jnp.int32),
      ng=vec(attn_norm_g), win=win_p, mu=vec(rw_shift_mu), wup=rw_w_up, w0=vec(rw_w0),
      aup=rw_a_up, a0=vec(rw_a0), gup=rw_g_up, kk=vec(rw_k_k), ka=vec(rw_k_a),
      rk=vec(rw_r_k), lng=vec(rw_ln_g), lnb=vec(rw_ln_b),
      cqg=vec(mla_cq_norm_g), ckvg=vec(mla_ckv_norm_g),
      wuq=_pad_heads(mla_w_uq, MLA_HEADS, MLA_QK_DIM, MLA_SLOT).astype(BF16),
      wuk=wuk.astype(BF16), wuv=wuv.astype(BF16),
      qg=head_gain(mla_q_norm_g), kg=head_gain(mla_k_norm_g),
      wo=w_o.astype(BF16), fg=vec(ffn_norm_g),
      wg=ffn_w_gate.astype(BF16), wu=ffn_w_up.astype(BF16), wd=ffn_w_down.astype(BF16))


def _layer_spec(a, l, **kwargs):
  zeros = (0,) * (a.ndim - 1)
  return pl.BlockSpec((None,) + a.shape[1:], lambda *_: (l,) + zeros, **kwargs)


def kernel(x, positions, attn_norm_g, w_in, rw_shift_mu, rw_w_up, rw_w0, rw_a_up, rw_a0, rw_g_up, rw_k_k, rw_k_a, rw_r_k, rw_ln_g, rw_ln_b, mla_cq_norm_g, mla_ckv_norm_g, mla_w_uq, mla_w_ukv, mla_q_norm_g, mla_k_norm_g, w_o, ffn_norm_g, ffn_w_gate, ffn_w_up, ffn_w_down):
  b, t, d = x.shape
  depth = w_in.shape[0]
  params = (attn_norm_g, w_in, rw_shift_mu, rw_w_up, rw_w0, rw_a_up, rw_a0, rw_g_up, rw_k_k, rw_k_a, rw_r_k,
            rw_ln_g, rw_ln_b, mla_cq_norm_g, mla_ckv_norm_g, mla_w_uq, mla_w_ukv, mla_q_norm_g, mla_k_norm_g,
            w_o, ffn_norm_g, ffn_w_gate, ffn_w_up, ffn_w_down)
  tile = min(TOKEN_TILE, t)
  assert t % tile == 0 and tile % SB_TILE == 0 and tile % CHUNK == 0
  cos, sin = _rope_tables(positions, tile)
  ws = _stack_weights(*params)
  for l in range(depth):
    (r, lwd, k, v, kkn, bvec, g, mq, mk, mv, sq, sk, sv) = _inproj(x, cos, sin, ws, l, tile)
    y_rw = _rwkv(r, lwd, k, v, kkn, bvec, g, ws, l, tc=tile)
    y_mla = _mla(ws["fixed_shift"][l], mq, mk, mv, tq=tile, tk=tile)
    y_sb = _sb(sq, sk, sv, tq=tile, tk=SB_TILE)
    x = _outffn(x.reshape(b * t, d), y_rw.reshape(b * t, -1), y_mla.reshape(b * t, -1),
                y_sb.reshape(b * t, -1), ws, l, tm=tile).reshape(b, t, d)
  return x
```
